```python
import numpy as np
import jax
import jax.numpy as jnp
from jax import lax

D_MODEL = 2048
BATCH = 16
SEQ = 2048
DEPTH = 2

HEAD_DIM = 128
A_GROUPS = 8
A_WIDTH = A_GROUPS * HEAD_DIM
A_CHUNK = 128
B_HEADS = 8
B_KV_HEADS = 2
B_GROUP = B_HEADS // B_KV_HEADS
B_WIDTH = B_HEADS * HEAD_DIM
KV_WIDTH = B_KV_HEADS * HEAD_DIM
CMP_LEN = 32
CMP_STRIDE = 16
CMP_HIDDEN = 256
SLC_BLOCK = 64
TOP_N = 8
WINDOW = 512
Q_BLOCK = 128
SLC_Q_BLOCK = 32
EVEN_IN = 3 * A_WIDTH + 2 * B_WIDTH + 6 * KV_WIDTH + 3 * B_HEADS
C_HEADS = 16
C_KDIM = 128
C_VDIM = 128
C_KW = C_HEADS * C_KDIM
C_WIDTH = C_HEADS * C_VDIM
C_CHUNK = 64
ODD_IN = 2 * C_KW + 2 * C_WIDTH

N_EVEN = (DEPTH + 1) // 2
N_ODD = DEPTH // 2
EPS = 1e-6
NEG_INF = -1e30
FORCE_SCORE = 1e6

kernel_name = "hybrid_gmlp_nsa_hgrn2_trunk"


def rms_norm(x, g):
    xf = x.astype(jnp.float32)
    y = xf * lax.rsqrt(jnp.mean(xf * xf, axis=-1, keepdims=True) + EPS)
    return (y * g.astype(jnp.float32)).astype(x.dtype)


def layer_norm(x, g, b):
    xf = x.astype(jnp.float32)
    mu = jnp.mean(xf, axis=-1, keepdims=True)
    var = jnp.mean(jnp.square(xf - mu), axis=-1, keepdims=True)
    y = (xf - mu) * lax.rsqrt(var + EPS)
    return (y * g.astype(jnp.float32) + b.astype(jnp.float32)).astype(x.dtype)


def masked_softmax(s, valid):
    s = jnp.where(valid, s.astype(jnp.float32), NEG_INF)
    m = jnp.max(s, axis=-1, keepdims=True)
    e = jnp.where(valid, jnp.exp(s - m), 0.0)
    return e / jnp.maximum(jnp.sum(e, axis=-1, keepdims=True), 1e-30)


def alibi_slopes():
    h = jnp.arange(1, B_HEADS + 1, dtype=jnp.float32)
    return jnp.exp2(-8.0 * h / B_HEADS).reshape(B_KV_HEADS, B_GROUP)


def cmp_to_slc_overlap(n_cmp, n_slc):
    cs = np.arange(n_cmp)[:, None] * CMP_STRIDE
    ss = np.arange(n_slc)[None, :] * SLC_BLOCK
    ov = np.clip(np.minimum(cs + CMP_LEN, ss + SLC_BLOCK) - np.maximum(cs, ss), 0, None) / CMP_LEN
    return jnp.asarray(ov, dtype=jnp.float32)


def compress(k, blk_idx, pos, w1, b1, w2):
    B = k.shape[0]
    n_cmp = blk_idx.shape[0]
    blocks = k[:, blk_idx] + pos[:, None, :]
    flat = blocks.transpose(0, 1, 3, 2, 4).reshape(B, n_cmp, B_KV_HEADS, CMP_LEN * HEAD_DIM)
    return jax.nn.gelu(flat @ w1 + b1) @ w2


def nsa_attention(q, k_cmp, v_cmp, k_slc, v_slc, k_win, v_win, gates,
                  q_gain, k_gain, cmp_pos, cmp_w1, cmp_b1, cmp_w2):
    B, S = q.shape[:2]
    slopes = alibi_slopes()
    q = rms_norm(q, q_gain) * (HEAD_DIM ** -0.5)
    t = jnp.arange(S)

    n_cmp = (S - CMP_LEN) // CMP_STRIDE + 1
    blk_idx = np.arange(n_cmp)[:, None] * CMP_STRIDE + np.arange(CMP_LEN)[None, :]
    kc = rms_norm(compress(k_cmp, blk_idx, cmp_pos[0], cmp_w1[0], cmp_b1[0], cmp_w2[0]), k_gain[0])
    vc = compress(v_cmp, blk_idx, cmp_pos[1], cmp_w1[1], cmp_b1[1], cmp_w2[1])
    end = jnp.asarray(blk_idx[:, -1])
    dist_c = t[:, None] - end[None, :]
    s_c = jnp.einsum('bskgd,bnkd->bkgsn', q, kc)
    s_c = s_c - slopes[:, :, None, None] * dist_c.astype(jnp.float32)
    p_cmp = masked_softmax(s_c, dist_c >= 0)
    o_cmp = jnp.einsum('bkgsn,bnkd->bskgd', p_cmp.astype(vc.dtype), vc)

    n_slc = S // SLC_BLOCK
    n_sel = min(TOP_N, n_slc)
    imp = jnp.einsum('bkgsn,nj->bksj', p_cmp, cmp_to_slc_overlap(n_cmp, n_slc))
    j = jnp.arange(n_slc)[None, :]
    cur = (t // SLC_BLOCK)[:, None]
    valid_blk = j * SLC_BLOCK <= t[:, None]
    forced = (j == 0) | (j == cur) | (j == cur - 1)
    imp = jnp.where(forced, FORCE_SCORE, jnp.where(valid_blk, imp, -1.0))
    _, sel = lax.top_k(imp, n_sel)

    ks_blocks = rms_norm(k_slc, k_gain[1]).reshape(B, n_slc, SLC_BLOCK, B_KV_HEADS, HEAD_DIM).transpose(0, 3, 1, 2, 4)
    vs_blocks = v_slc.reshape(B, n_slc, SLC_BLOCK, B_KV_HEADS, HEAD_DIM).transpose(0, 3, 1, 2, 4)
    nqb = S // SLC_Q_BLOCK
    q_sb = q.reshape(B, nqb, SLC_Q_BLOCK, B_KV_HEADS, B_GROUP, HEAD_DIM).swapaxes(0, 1)
    sel_b = sel.reshape(B, B_KV_HEADS, nqb, SLC_Q_BLOCK, n_sel).transpose(2, 0, 1, 3, 4)
    t_b = t.reshape(nqb, SLC_Q_BLOCK)
    b_ix = jnp.arange(B)[:, None, None, None]
    k_ix = jnp.arange(B_KV_HEADS)[None, :, None, None]
    n_keys = n_sel * SLC_BLOCK

    def slc_block(args):
        qb, sb, tb = args
        kg = ks_blocks[b_ix, k_ix, sb]
        vg = vs_blocks[b_ix, k_ix, sb].reshape(B, B_KV_HEADS, SLC_Q_BLOCK, n_keys, HEAD_DIM)
        kpos = sb[..., None] * SLC_BLOCK + jnp.arange(SLC_BLOCK)
        dist = (tb[None, None, :, None, None] - kpos).reshape(B, B_KV_HEADS, 1, SLC_Q_BLOCK, n_keys)
        sc = jnp.einsum('bqkgd,bkqnjd->bkgqnj', qb, kg).reshape(B, B_KV_HEADS, B_GROUP, SLC_Q_BLOCK, n_keys)
        sc = sc - slopes[None, :, :, None, None] * dist.astype(jnp.float32)
        p = masked_softmax(sc, dist >= 0)
        return jnp.einsum('bkgqm,bkqmd->bqkgd', p.astype(vg.dtype), vg)

    o_slc = lax.map(slc_block, (q_sb, sel_b, t_b)).swapaxes(0, 1).reshape(B, S, B_KV_HEADS, B_GROUP, HEAD_DIM)

    pad = ((0, 0), (WINDOW, 0), (0, 0), (0, 0))
    kw_p = jnp.pad(rms_norm(k_win, k_gain[2]), pad)
    vw_p = jnp.pad(v_win, pad)
    nwb = S // Q_BLOCK
    span = WINDOW + Q_BLOCK
    q_wb = q.reshape(B, nwb, Q_BLOCK, B_KV_HEADS, B_GROUP, HEAD_DIM).swapaxes(0, 1)

    def win_block(args):
        i, qb = args
        kb = lax.dynamic_slice_in_dim(kw_p, i * Q_BLOCK, span, axis=1)
        vb = lax.dynamic_slice_in_dim(vw_p, i * Q_BLOCK, span, axis=1)
        tq = i * Q_BLOCK + jnp.arange(Q_BLOCK)
        kpos = i * Q_BLOCK - WINDOW + jnp.arange(span)
        dist = tq[:, None] - kpos[None, :]
        valid = (kpos[None, :] >= 0) & (dist >= 0) & (dist < WINDOW)
        sc = jnp.einsum('bqkgd,bmkd->bkgqm', qb, kb) - slopes[:, :, None, None] * dist.astype(jnp.float32)
        p = masked_softmax(sc, valid)
        return jnp.einsum('bkgqm,bmkd->bqkgd', p.astype(vb.dtype), vb)

    o_win = lax.map(win_block, (jnp.arange(nwb), q_wb)).swapaxes(0, 1).reshape(B, S, B_KV_HEADS, B_GROUP, HEAD_DIM)

    o = (gates[:, :, 0, :, :, None] * o_cmp + gates[:, :, 1, :, :, None] * o_slc
         + gates[:, :, 2, :, :, None] * o_win)
    return o.reshape(B, S, B_WIDTH)


def even_layer(x, norm_g, w_in, ln_g, ln_b, sgu_w, sgu_b, q_gain, k_gain,
               cmp_pos, cmp_w1, cmp_b1, cmp_w2, w_out):
    B, S, _ = x.shape
    z = rms_norm(x, norm_g) @ w_in
    cuts = (A_WIDTH, 2 * A_WIDTH, 3 * A_WIDTH, 3 * A_WIDTH + B_WIDTH,
            3 * A_WIDTH + 2 * B_WIDTH, 3 * A_WIDTH + 2 * B_WIDTH + 6 * KV_WIDTH)
    a_u, a_v, a_gate, b_q, b_gate, b_kv, b_g = jnp.split(z, cuts, axis=-1)

    u = jax.nn.gelu(a_u)
    v = layer_norm(jax.nn.gelu(a_v), ln_g, ln_b)
    nca = S // A_CHUNK
    v = v.reshape(B, nca, A_CHUNK, A_GROUPS, HEAD_DIM)
    w_s = jnp.where(np.tril(np.ones((A_CHUNK, A_CHUNK), dtype=bool)), sgu_w, 0.0)
    v = jnp.einsum('gts,bnsgc->bntgc', w_s, v) + sgu_b.T[:, :, None]
    a_out = u * v.reshape(B, S, A_WIDTH) * jax.nn.silu(a_gate)

    q = b_q.reshape(B, S, B_KV_HEADS, B_GROUP, HEAD_DIM)
    kv = b_kv.reshape(B, S, 6, B_KV_HEADS, HEAD_DIM)
    gates = jax.nn.sigmoid(b_g.reshape(B, S, 3, B_KV_HEADS, B_GROUP))
    b_out = nsa_attention(q, kv[:, :, 0], kv[:, :, 1], kv[:, :, 2], kv[:, :, 3], kv[:, :, 4], kv[:, :, 5],
                          gates, q_gain, k_gain, cmp_pos, cmp_w1, cmp_b1, cmp_w2)
    b_out = b_out * jax.nn.silu(b_gate)

    return x + jnp.concatenate([a_out, b_out], axis=-1) @ w_out


def hgrn2_chunked(q, k, g, v):
    B, S, _ = q.shape
    nc = S // C_CHUNK

    def heads(a, d):
        return a.reshape(B, nc, C_CHUNK, C_HEADS, d).transpose(0, 3, 1, 2, 4)

    q, k, g, v = heads(q, C_KDIM), heads(k, C_KDIM), heads(g, C_KDIM), heads(v, C_VDIM)
    G = jnp.cumsum(g, axis=3)
    G_mid = G[:, :, :, C_CHUNK // 2 - 1:C_CHUNK // 2]
    G_end = G[:, :, :, -1:]
    attn = jnp.einsum('bhntd,bhnsd->bhnts', q * jnp.exp(G - G_mid), k * jnp.exp(G_mid - G))
    attn = jnp.where(np.tril(np.ones((C_CHUNK, C_CHUNK), dtype=bool)), attn, 0.0)
    o_intra = jnp.einsum('bhnts,bhnsv->bhntv', attn, v)
    q_in = jnp.moveaxis(q * jnp.exp(G), 2, 0)
    k_st = jnp.moveaxis(k * jnp.exp(G_end - G), 2, 0)
    v_c = jnp.moveaxis(v, 2, 0)
    decay = jnp.moveaxis(jnp.exp(G_end[:, :, :, 0]), 2, 0)

    def step(state, xs):
        qn, kn, vn, dn = xs
        o = jnp.einsum('bhtd,bhdv->bhtv', qn, state)
        state = dn[..., None] * state + jnp.einsum('bhsd,bhsv->bhdv', kn, vn)
        return state, o

    init = jnp.zeros((B, C_HEADS, C_KDIM, C_VDIM), q.dtype)
    _, o_inter = lax.scan(step, init, (q_in, k_st, v_c, decay))
    o = o_intra + jnp.moveaxis(o_inter, 0, 2)
    return o.transpose(0, 2, 3, 1, 4).reshape(B, S, C_HEADS, C_VDIM)


def odd_layer(x, layer, norm_g, w_in, lb_logits, out_g, w_out):
    B, S, _ = x.shape
    z = rms_norm(x, norm_g) @ w_in
    q, f_logit, i_in, gate = jnp.split(z, (C_KW, 2 * C_KW, 2 * C_KW + C_WIDTH), axis=-1)
    p = jax.nn.softmax(lb_logits.astype(jnp.float32), axis=0)
    lb = (jnp.cumsum(p, axis=0) - p[0])[layer]
    f = lb + (1.0 - lb) * jax.nn.sigmoid(f_logit.astype(jnp.float32))
    o = hgrn2_chunked(q.astype(jnp.float32), 1.0 - f, jnp.log(f), i_in.astype(jnp.float32))
    o = rms_norm(o, out_g).reshape(B, S, C_WIDTH).astype(x.dtype) * jax.nn.silu(gate)
    return x + o @ w_out


def setup_inputs(seed: int = 0) -> dict:
    key = jax.random.key(seed)
    ks = jax.random.split(key, 20)

    def nrm(k, shape, scale):
        return scale * jax.random.normal(k, shape, jnp.float32)

    ab = A_WIDTH + B_WIDTH
    return {
        "x": nrm(ks[0], (BATCH, SEQ, D_MODEL), 1.0),
        "even_norm": 1.0 + nrm(ks[1], (N_EVEN, D_MODEL), 0.02),
        "even_w_in": nrm(ks[2], (N_EVEN, D_MODEL, EVEN_IN), D_MODEL ** -0.5),
        "sgu_ln_g": 1.0 + nrm(ks[3], (N_EVEN, A_WIDTH), 0.02),
        "sgu_ln_b": nrm(ks[4], (N_EVEN, A_WIDTH), 0.02),
        "sgu_w": nrm(ks[5], (N_EVEN, A_GROUPS, A_CHUNK, A_CHUNK), A_CHUNK ** -0.5),
        "sgu_b": 1.0 + nrm(ks[6], (N_EVEN, A_GROUPS, A_CHUNK), 0.1),
        "nsa_q_gain": 1.0 + nrm(ks[7], (N_EVEN, HEAD_DIM), 0.02),
        "nsa_k_gain": 1.0 + nrm(ks[8], (N_EVEN, 3, HEAD_DIM), 0.02),
        "cmp_pos": nrm(ks[9], (N_EVEN, 2, CMP_LEN, HEAD_DIM), 0.02),
        "cmp_w1": nrm(ks[10], (N_EVEN, 2, CMP_LEN * HEAD_DIM, CMP_HIDDEN), (CMP_LEN * HEAD_DIM) ** -0.5),
        "cmp_b1": nrm(ks[11], (N_EVEN, 2, CMP_HIDDEN), 0.01),
        "cmp_w2": nrm(ks[12], (N_EVEN, 2, CMP_HIDDEN, HEAD_DIM), CMP_HIDDEN ** -0.5),
        "even_w_out": nrm(ks[13], (N_EVEN, ab, D_MODEL), ab ** -0.5),
        "odd_norm": 1.0 + nrm(ks[14], (N_ODD, D_MODEL), 0.02),
        "odd_w_in": nrm(ks[15], (N_ODD, D_MODEL, ODD_IN), D_MODEL ** -0.5),
        "hgrn_lb": nrm(ks[16], (DEPTH, C_KW), 0.1),
        "hgrn_out_gain": 1.0 + nrm(ks[17], (N_ODD, C_VDIM), 0.02),
        "odd_w_out": nrm(ks[18], (N_ODD, C_WIDTH, D_MODEL), C_WIDTH ** -0.5),
    }


def reference(x, even_norm, even_w_in, sgu_ln_g, sgu_ln_b, sgu_w, sgu_b, nsa_q_gain, nsa_k_gain,
              cmp_pos, cmp_w1, cmp_b1, cmp_w2, even_w_out, odd_norm, odd_w_in, hgrn_lb,
              hgrn_out_gain, odd_w_out):
    for layer in range(DEPTH):
        if layer % 2 == 0:
            e = layer // 2
            x = even_layer(x, even_norm[e], even_w_in[e], sgu_ln_g[e], sgu_ln_b[e], sgu_w[e], sgu_b[e],
                           nsa_q_gain[e], nsa_k_gain[e], cmp_pos[e], cmp_w1[e], cmp_b1[e], cmp_w2[e],
                           even_w_out[e])
        else:
            o = layer // 2
            x = odd_layer(x, layer, odd_norm[o], odd_w_in[o], hgrn_lb, hgrn_out_gain[o], odd_w_out[o])
    return x
```

```python
import functools

import numpy as np
import jax
import jax.numpy as jnp
from jax import lax
from jax.experimental import pallas as pl
from jax.experimental.pallas import tpu as pltpu

HEAD_DIM = 128
A_GROUPS = 8
A_WIDTH = A_GROUPS * HEAD_DIM
A_CHUNK = 128
B_HEADS = 8
B_KV_HEADS = 2
B_GROUP = B_HEADS // B_KV_HEADS
B_WIDTH = B_HEADS * HEAD_DIM
CMP_LEN = 32
CMP_STRIDE = 16
CMP_HIDDEN = 256
SLC_BLOCK = 64
TOP_N = 8
WINDOW = 512
C_HEADS = 16
C_DIM = 128
C_CHUNK = 64
EPS = 1e-6
NEG_INF = -1e30
FORCE_SCORE = 1e6

LANES = 128
VMEM_LIMIT_BYTES = 56 * 1024 * 1024

PROJ_TM = 1024
PROJ_TN = 512
NORM_ROWS = 256
MIXA_ROWS = 512
ATT_TQ = 256
HGRN_HEADS = 2

BF16 = jnp.bfloat16
F32 = jnp.float32


def _params(*sem):
    return pltpu.CompilerParams(dimension_semantics=sem, vmem_limit_bytes=VMEM_LIMIT_BYTES)


def _dot(a, b):
    return jnp.dot(a, b, preferred_element_type=F32)


def _dot_nt(a, b):
    return lax.dot_general(a, b, (((1,), (1,)), ((), ())), preferred_element_type=F32)


def _dot_tn(a, b):
    return lax.dot_general(a, b, (((0,), (0,)), ((), ())), preferred_element_type=F32)


def _rms(x, gain):
    return x * lax.rsqrt(jnp.mean(x * x, axis=-1, keepdims=True) + EPS) * gain


def _rms_matmul_kernel(x_ref, g_ref, w_ref, *rest, has_extra):
    if has_extra:
        w2_ref, o_ref, o2_ref, xn_ref = rest
    else:
        o_ref, xn_ref = rest

    @pl.when(pl.program_id(1) == 0)
    def _():
        def body(r, carry):
            r0 = pl.multiple_of(r * NORM_ROWS, NORM_ROWS)
            x = x_ref[pl.ds(r0, NORM_ROWS), :]
            xn_ref[pl.ds(r0, NORM_ROWS), :] = _rms(x, g_ref[...]).astype(BF16)
            return carry

        lax.fori_loop(0, x_ref.shape[0] // NORM_ROWS, body, 0)
        if has_extra:
            o2_ref[...] = _dot(xn_ref[...], w2_ref[...])

    o_ref[...] = _dot(xn_ref[...], w_ref[...])


def _rms_matmul(x, gain, w, w_extra=None):
    t, d = x.shape
    n = w.shape[1]
    tm = min(PROJ_TM, t)
    tn = min(PROJ_TN, n)
    assert t % tm == 0 and n % tn == 0 and tm % NORM_ROWS == 0
    in_specs = [
        pl.BlockSpec((tm, d), lambda i, j: (i, 0)),
        pl.BlockSpec((1, d), lambda i, j: (0, 0)),
        pl.BlockSpec((d, tn), lambda i, j: (0, j)),
    ]
    out_shape = [jax.ShapeDtypeStruct((t, n), F32)]
    out_specs = [pl.BlockSpec((tm, tn), lambda i, j: (i, j))]
    args = [x, gain.reshape(1, d), w]
    if w_extra is not None:
        n2 = w_extra.shape[1]
        in_specs.append(pl.BlockSpec((d, n2), lambda i, j: (0, 0)))
        out_shape.append(jax.ShapeDtypeStruct((t, n2), F32))
        out_specs.append(pl.BlockSpec((tm, n2), lambda i, j: (i, 0)))
        args.append(w_extra)
    out = pl.pallas_call(
        functools.partial(_rms_matmul_kernel, has_extra=w_extra is not None),
        grid=(t // tm, n // tn),
        in_specs=in_specs,
        out_specs=out_specs,
        out_shape=out_shape,
        scratch_shapes=[pltpu.VMEM((tm, d), BF16)],
        compiler_params=_params("parallel", "arbitrary"),
        name="rms_matmul",
    )(*args)
    return out if w_extra is not None else out[0]


def _matmul_residual_kernel(x_ref, *refs, n_pairs):
    acts, ws, o_ref = refs[:n_pairs], refs[n_pairs:2 * n_pairs], refs[2 * n_pairs]
    acc = x_ref[...]
    for a_ref, w_ref in zip(acts, ws):
        acc = acc + _dot(a_ref[...], w_ref[...])
    o_ref[...] = acc


def _matmul_residual(x, acts, ws):
    t, d = x.shape
    tm = min(PROJ_TM, t)
    tn = min(PROJ_TN, d)
    assert t % tm == 0 and d % tn == 0
    in_specs = [pl.BlockSpec((tm, tn), lambda i, j: (i, j))]
    in_specs += [pl.BlockSpec((tm, a.shape[1]), lambda i, j: (i, 0)) for a in acts]
    in_specs += [pl.BlockSpec((w.shape[0], tn), lambda i, j: (0, j)) for w in ws]
    return pl.pallas_call(
        functools.partial(_matmul_residual_kernel, n_pairs=len(acts)),
        grid=(t // tm, d // tn),
        in_specs=in_specs,
        out_specs=pl.BlockSpec((tm, tn), lambda i, j: (i, j)),
        out_shape=jax.ShapeDtypeStruct((t, d), F32),
        compiler_params=_params("parallel", "arbitrary"),
        name="matmul_residual",
    )(x, *acts, *ws)


def _mixer_a_kernel(u_ref, v_ref, gate_ref, lng_ref, lnb_ref, w_ref, b_ref, o_ref):
    rows = u_ref.shape[0]
    v = jax.nn.gelu(v_ref[...])
    mu = jnp.mean(v, axis=-1, keepdims=True)
    vc = v - mu
    var = jnp.mean(vc * vc, axis=-1, keepdims=True)
    vn = (vc * lax.rsqrt(var + EPS) * lng_ref[...] + lnb_ref[...]).astype(BF16)
    tr = lax.broadcasted_iota(jnp.int32, (A_CHUNK, A_CHUNK), 0)
    tc = lax.broadcasted_iota(jnp.int32, (A_CHUNK, A_CHUNK), 1)
    for g in range(A_GROUPS):
        cols = slice(g * HEAD_DIM, (g + 1) * HEAD_DIM)
        wm = jnp.where(tr >= tc, w_ref[g], 0.0).astype(BF16)
        for c in range(rows // A_CHUNK):
            rs = slice(c * A_CHUNK, (c + 1) * A_CHUNK)
            sv = _dot(wm, vn[rs, cols]) + b_ref[g]
            u = jax.nn.gelu(u_ref[rs, cols])
            o_ref[rs, cols] = (u * sv * jax.nn.silu(gate_ref[rs, cols])).astype(BF16)


def _mixer_a(z, ln_g, ln_b, sgu_w, sgu_b_full):
    t = z.shape[0]
    rows = min(MIXA_ROWS, t)
    assert t % rows == 0 and rows % A_CHUNK == 0
    blk = lambda c: pl.BlockSpec((rows, A_WIDTH), lambda i: (i, c))
    full = lambda a: pl.BlockSpec(a.shape, lambda i: (0,) * a.ndim)
    lng = ln_g.reshape(1, A_WIDTH)
    lnb = ln_b.reshape(1, A_WIDTH)
    return pl.pallas_call(
        _mixer_a_kernel,
        grid=(t // rows,),
        in_specs=[blk(0), blk(1), blk(2), full(lng), full(lnb), full(sgu_w), full(sgu_b_full)],
        out_specs=pl.BlockSpec((rows, A_WIDTH), lambda i: (i, 0)),
        out_shape=jax.ShapeDtypeStruct((t, A_WIDTH), BF16),
        compiler_params=_params("parallel"),
        name="mixer_a",
    )(z, z, z, lng, lnb, sgu_w, sgu_b_full)


def _compress(src_ref, pos_ref, w1_ref, b1_ref, w2_ref):
    groups = src_ref.shape[0] // CMP_STRIDE
    lo = jnp.zeros((groups, CMP_HIDDEN), F32)
    hi = jnp.zeros((groups, CMP_HIDDEN), F32)
    for l in range(CMP_STRIDE):
        xl = src_ref[pl.ds(l, groups, stride=CMP_STRIDE), :]
        lo = lo + _dot((xl + pos_ref[l:l + 1, :]).astype(BF16), w1_ref[l])
        hi = hi + _dot((xl + pos_ref[CMP_STRIDE + l:CMP_STRIDE + l + 1, :]).astype(BF16),
                       w1_ref[CMP_STRIDE + l])
    hid = jax.nn.gelu(lo + pltpu.roll(hi, groups - 1, 0) + b1_ref[...])
    return _dot(hid.astype(BF16), w2_ref[...])


def _kv_prep_kernel(kc_src, vc_src, ks_src, vs_src, kw_src, vw_src, kgain_ref, pos_ref, w1_ref,
                    b1_ref, w2_ref, kc_ref, vc_ref, ks_ref, vs_ref, kw_ref, vw_ref):
    ks_ref[...] = _rms(ks_src[...], kgain_ref[1:2, :]).astype(BF16)
    kw_ref[...] = _rms(kw_src[...], kgain_ref[2:3, :]).astype(BF16)
    vs_ref[...] = vs_src[...].astype(BF16)
    vw_ref[...] = vw_src[...].astype(BF16)
    kc = _compress(kc_src, pos_ref.at[0], w1_ref.at[0], b1_ref.at[0], w2_ref.at[0])
    kc_ref[...] = _rms(kc, kgain_ref[0:1, :])
    vc_ref[...] = _compress(vc_src, pos_ref.at[1], w1_ref.at[1], b1_ref.at[1], w2_ref.at[1])


def _kv_prep(z, batch, seq, kv_col0, k_gain, cmp_pos, w1, b1, w2):
    groups = seq // CMP_STRIDE
    src = lambda i: pl.BlockSpec((seq, HEAD_DIM), lambda b, h, i=i: (b, kv_col0 + i * B_KV_HEADS + h))
    full = lambda a: pl.BlockSpec(a.shape, lambda b, h: (0,) * a.ndim)
    small = pl.BlockSpec((None, None, groups, HEAD_DIM), lambda b, h: (b, h, 0, 0))
    big = pl.BlockSpec((None, None, seq, HEAD_DIM), lambda b, h: (b, h, 0, 0))
    small_shape = jax.ShapeDtypeStruct((batch, B_KV_HEADS, groups, HEAD_DIM), F32)
    big_shape = jax.ShapeDtypeStruct((batch, B_KV_HEADS, seq, HEAD_DIM), BF16)
    return pl.pallas_call(
        _kv_prep_kernel,
        grid=(batch, B_KV_HEADS),
        in_specs=[src(i) for i in range(6)] + [full(k_gain), full(cmp_pos), full(w1), full(b1), full(w2)],
        out_specs=[small, small, big, big, big, big],
        out_shape=[small_shape, small_shape, big_shape, big_shape, big_shape, big_shape],
        compiler_params=_params("parallel", "parallel"),
        name="kv_prep",
    )(z, z, z, z, z, z, k_gain, cmp_pos, w1, b1, w2)


def _online_update(s, v_blk, m_ref, l_ref, acc_ref, g):
    m_old = m_ref[g]
    m_new = jnp.maximum(m_old, jnp.max(s, axis=-1, keepdims=True))
    alpha = jnp.exp(m_old - m_new)
    p = jnp.exp(s - m_new)
    l_ref[g] = alpha * l_ref[g] + jnp.sum(p, axis=-1, keepdims=True)
    acc_ref[g] = alpha * acc_ref[g] + _dot(p.astype(BF16), v_blk)
    m_ref[g] = m_new


def _nsa_kernel(slopes_ref, q_ref, bg_ref, gl_ref, kc_ref, vc_ref, ks_ref, vs_ref, kw_ref, vw_ref,
                qgain_ref, ovt_ref, o_ref, qn_ref, m_ref, l_ref, acc_ref, *, n_cmp):
    tq = q_ref.shape[0]
    n_grp = kc_ref.shape[0]
    n_slc = ovt_ref.shape[0]
    h = pl.program_id(1)
    i = pl.program_id(2)
    t0 = i * tq

    for g in range(B_GROUP):
        q = q_ref[:, g * HEAD_DIM:(g + 1) * HEAD_DIM]
        qn_ref[g] = (_rms(q, qgain_ref[...]) * (HEAD_DIM ** -0.5)).astype(BF16)

    t_col = t0 + lax.broadcasted_iota(jnp.int32, (tq, n_grp), 0)
    n_row = lax.broadcasted_iota(jnp.int32, (tq, n_grp), 1)
    dist_c = t_col - (n_row * CMP_STRIDE + (CMP_LEN - 1))
    valid_c = (n_row < n_cmp) & (dist_c >= 0)
    dist_cf = dist_c.astype(F32)
    kc = kc_ref[...].astype(BF16)
    vc = vc_ref[...].astype(BF16)
    p_sum = jnp.zeros((tq, n_grp), F32)
    o_cmp = []
    for g in range(B_GROUP):
        slope = slopes_ref[h * B_GROUP + g]
        s = _dot_nt(qn_ref[g], kc) - slope * dist_cf
        s = jnp.where(valid_c, s, NEG_INF)
        m = jnp.max(s, axis=-1, keepdims=True)
        e = jnp.where(valid_c, jnp.exp(s - m), 0.0)
        p = e / jnp.maximum(jnp.sum(e, axis=-1, keepdims=True), 1e-30)
        p_sum = p_sum + p
        o_cmp.append(_dot(p.astype(BF16), vc))

    imp = lax.dot_general(ovt_ref[...], p_sum, (((1,), (1,)), ((), ())),
                          precision=lax.Precision.HIGHEST, preferred_element_type=F32)
    j_idx = lax.broadcasted_iota(jnp.int32, (n_slc, tq), 0)
    j_f = j_idx.astype(F32)
    t_row = t0 + lax.broadcasted_iota(jnp.int32, (n_slc, tq), 1)
    cur = t_row // SLC_BLOCK
    forced = (j_idx == 0) | (j_idx == cur) | (j_idx == cur - 1)
    val = jnp.where(forced, FORCE_SCORE, jnp.where(j_idx * SLC_BLOCK <= t_row, imp, -1.0))
    sel_t = jnp.zeros((n_slc, tq), F32)
    for _ in range(min(TOP_N, n_slc)):
        best = jnp.max(val, axis=0, keepdims=True)
        first = jnp.min(jnp.where(val == best, j_f, float(n_slc)), axis=0, keepdims=True)
        hit = j_f == first
        sel_t = jnp.where(hit, 1.0, sel_t)
        val = jnp.where(hit, -jnp.inf, val)
    sel = jnp.concatenate([sel_t, jnp.zeros((LANES - n_slc, tq), F32)], axis=0).T.astype(BF16)

    row = lax.broadcasted_iota(jnp.int32, (tq, tq), 0)
    col = lax.broadcasted_iota(jnp.int32, (tq, tq), 1)
    causal = col <= row
    rel_diag = col - row
    blk_j = lax.broadcasted_iota(jnp.int32, (LANES, tq), 0)
    blk_c = lax.broadcasted_iota(jnp.int32, (LANES, tq), 1)

    def sel_mask(kt):
        expand = jnp.where((kt * tq + blk_c) // SLC_BLOCK == blk_j, 1.0, 0.0).astype(BF16)
        return _dot(sel, expand) > 0.5

    def run_branch(k_ref, v_ref, diag_mask, lo_tile, off_mask_fn):
        k_blk = k_ref[pl.ds(pl.multiple_of(t0, tq), tq), :]
        v_blk = v_ref[pl.ds(pl.multiple_of(t0, tq), tq), :]
        rel_f = rel_diag.astype(F32)
        for g in range(B_GROUP):
            slope = slopes_ref[h * B_GROUP + g]
            s = _dot_nt(qn_ref[g], k_blk) + slope * rel_f
            s = jnp.where(diag_mask, s, NEG_INF)
            m = jnp.max(s, axis=-1, keepdims=True)
            p = jnp.exp(s - m)
            m_ref[g] = m
            l_ref[g] = jnp.sum(p, axis=-1, keepdims=True)
            acc_ref[g] = _dot(p.astype(BF16), v_blk)

        def body(kt, carry):
            k0 = pl.multiple_of(kt * tq, tq)
            k_blk = k_ref[pl.ds(k0, tq), :]
            v_blk = v_ref[pl.ds(k0, tq), :]
            mask = off_mask_fn(kt)
            rel = (rel_diag + (k0 - t0)).astype(F32)
            for g in range(B_GROUP):
                slope = slopes_ref[h * B_GROUP + g]
                s = _dot_nt(qn_ref[g], k_blk) + slope * rel
                if mask is not None:
                    s = jnp.where(mask, s, NEG_INF)
                _online_update(s, v_blk, m_ref, l_ref, acc_ref, g)
            return carry

        lax.fori_loop(lo_tile, i, body, 0)
        return [acc_ref[g] / l_ref[g] for g in range(B_GROUP)]

    o_slc = run_branch(ks_ref, vs_ref, causal & sel_mask(i), 0, sel_mask)

    def win_mask(kt):
        return (row - col) + (i - kt) * tq < WINDOW

    win_tiles = (WINDOW + tq - 1) // tq
    o_win = run_branch(kw_ref, vw_ref, causal, jnp.maximum(i - win_tiles, 0), win_mask)

    gates = jax.nn.sigmoid(gl_ref[...])
    for g in range(B_GROUP):
        c = slice(g * HEAD_DIM, (g + 1) * HEAD_DIM)
        o = (gates[:, g:g + 1] * o_cmp[g]
             + gates[:, B_GROUP + g:B_GROUP + g + 1] * o_slc[g]
             + gates[:, 2 * B_GROUP + g:2 * B_GROUP + g + 1] * o_win[g])
        o_ref[:, c] = (o * jax.nn.silu(bg_ref[:, c])).astype(BF16)


def _overlap_t(n_grp, n_cmp, n_slc):
    cs = np.arange(n_grp)[None, :] * CMP_STRIDE
    ss = np.arange(n_slc)[:, None] * SLC_BLOCK
    ov = np.clip(np.minimum(cs + CMP_LEN, ss + SLC_BLOCK) - np.maximum(cs, ss), 0, None) / CMP_LEN
    ov = ov * (np.arange(n_grp)[None, :] < n_cmp)
    return jnp.asarray(ov, dtype=F32)


def _nsa_attn(z, zg, kc, vc, ks, vs, kw, vw, q_gain, batch, seq, q_col0, bg_col0):
    t = z.shape[0]
    tq = min(ATT_TQ, seq)
    assert seq % tq == 0 and tq % SLC_BLOCK == 0 and seq // SLC_BLOCK <= LANES
    nq = seq // tq
    n_grp = seq // CMP_STRIDE
    n_cmp = (seq - CMP_LEN) // CMP_STRIDE + 1
    n_slc = seq // SLC_BLOCK
    gw = B_GROUP * HEAD_DIM
    h_idx = jnp.arange(1, B_HEADS + 1, dtype=F32)
    slopes = jnp.exp2(-8.0 * h_idx / B_HEADS)
    ovt = _overlap_t(n_grp, n_cmp, n_slc)
    rows = lambda b, h, i: b * nq + i
    small = pl.BlockSpec((None, None, n_grp, HEAD_DIM), lambda b, h, i: (b, h, 0, 0))
    big = pl.BlockSpec((None, None, seq, HEAD_DIM), lambda b, h, i: (b, h, 0, 0))
    full = lambda a: pl.BlockSpec(a.shape, lambda b, h, i: (0,) * a.ndim)
    qg = q_gain.reshape(1, HEAD_DIM)
    return pl.pallas_call(
        functools.partial(_nsa_kernel, n_cmp=n_cmp),
        grid=(batch, B_KV_HEADS, nq),
        in_specs=[
            pl.BlockSpec(memory_space=pltpu.SMEM),
            pl.BlockSpec((tq, gw), lambda b, h, i: (rows(b, h, i), q_col0 + h)),
            pl.BlockSpec((tq, gw), lambda b, h, i: (rows(b, h, i), bg_col0 + h)),
            pl.BlockSpec((tq, LANES), lambda b, h, i: (rows(b, h, i), h)),
            small, small, big, big, big, big, full(qg), full(ovt),
        ],
        out_specs=pl.BlockSpec((tq, gw), lambda b, h, i: (rows(b, h, i), h)),
        out_shape=jax.ShapeDtypeStruct((t, B_WIDTH), BF16),
        scratch_shapes=[
            pltpu.VMEM((B_GROUP, tq, HEAD_DIM), BF16),
            pltpu.VMEM((B_GROUP, tq, 1), F32),
            pltpu.VMEM((B_GROUP, tq, 1), F32),
            pltpu.VMEM((B_GROUP, tq, HEAD_DIM), F32),
        ],
        compiler_params=_params("parallel", "parallel", "arbitrary"),
        name="nsa_attn",
    )(slopes, z, z, zg, kc, vc, ks, vs, kw, vw, qg, ovt)


def _hgrn_kernel(q_ref, f_ref, i_ref, gate_ref, lb_ref, og_ref, o_ref, state_ref, *, layer):
    seq = q_ref.shape[0]
    n_heads = q_ref.shape[1] // C_DIM
    mid = C_CHUNK // 2 - 1

    logits = lb_ref[...]
    e = jnp.exp(logits - jnp.max(logits, axis=0, keepdims=True))
    p = e / jnp.sum(e, axis=0, keepdims=True)
    lb = jnp.sum(p[:layer + 1], axis=0, keepdims=True) - p[0:1]

    r_i = lax.broadcasted_iota(jnp.int32, (C_CHUNK, C_CHUNK), 0)
    c_i = lax.broadcasted_iota(jnp.int32, (C_CHUNK, C_CHUNK), 1)
    lower = r_i >= c_i
    tri = jnp.where(lower, 1.0, 0.0)
    state_ref[...] = jnp.zeros_like(state_ref)

    def body(c, carry):
        r0 = pl.multiple_of(c * C_CHUNK, C_CHUNK)
        rows = pl.ds(r0, C_CHUNK)
        f = lb + (1.0 - lb) * jax.nn.sigmoid(f_ref[rows, :])
        k = 1.0 - f
        gcum = jnp.dot(tri, jnp.log(f), precision=lax.Precision.HIGHEST, preferred_element_type=F32)
        g_mid = gcum[mid:mid + 1, :]
        g_end = gcum[C_CHUNK - 1:C_CHUNK, :]
        qa = q_ref[rows, :] * jnp.exp(gcum - g_mid)
        kb = k * jnp.exp(g_mid - gcum)
        q_in = (qa * jnp.exp(g_mid)).astype(BF16)
        k_st = (kb * jnp.exp(g_end - g_mid)).astype(BF16)
        qa = qa.astype(BF16)
        kb = kb.astype(BF16)
        v = i_ref[rows, :].astype(BF16)
        decay = jnp.exp(g_end)
        for hh in range(n_heads):
            cs = slice(hh * C_DIM, (hh + 1) * C_DIM)
            attn = jnp.where(lower, _dot_nt(qa[:, cs], kb[:, cs]), 0.0)
            state = state_ref[hh]
            o = _dot(attn.astype(BF16), v[:, cs]) + _dot(q_in[:, cs], state.astype(BF16))
            decay_rows = jnp.broadcast_to(decay[:, cs], (C_DIM, C_DIM)).T
            state_ref[hh] = decay_rows * state + _dot_tn(k_st[:, cs], v[:, cs])
            o = _rms(o, og_ref[...]) * jax.nn.silu(gate_ref[rows, cs])
            o_ref[rows, cs] = o.astype(BF16)
        return carry

    lax.fori_loop(0, seq // C_CHUNK, body, 0)


def _hgrn(z, lb_logits, out_gain, batch, seq, layer):
    t = z.shape[0]
    width = C_HEADS * C_DIM
    bw = HGRN_HEADS * C_DIM
    nb = width // bw
    depth = lb_logits.shape[0]
    col = lambda part: pl.BlockSpec((seq, bw), lambda b, j, part=part: (b, part * nb + j))
    og = out_gain.reshape(1, C_DIM)
    return pl.pallas_call(
        functools.partial(_hgrn_kernel, layer=layer),
        grid=(batch, nb),
        in_specs=[col(0), col(1), col(2), col(3),
                  pl.BlockSpec((depth, bw), lambda b, j: (0, j)),
                  pl.BlockSpec((1, C_DIM), lambda b, j: (0, 0))],
        out_specs=pl.BlockSpec((seq, bw), lambda b, j: (b, j)),
        out_shape=jax.ShapeDtypeStruct((t, width), BF16),
        scratch_shapes=[pltpu.VMEM((HGRN_HEADS, C_DIM, C_DIM), F32)],
        compiler_params=_params("parallel", "parallel"),
        name="hgrn",
    )(z, z, z, z, lb_logits, og)


def _even_layer(h, batch, seq, norm_g, w_in, ln_g, ln_b, sgu_w, sgu_b, q_gain, k_gain,
                cmp_pos, cmp_w1, cmp_b1, cmp_w2, w_out):
    d = h.shape[1]
    n_main = 3 * A_WIDTH + 2 * B_WIDTH + 6 * B_KV_HEADS * HEAD_DIM
    wg = w_in[:, n_main:].reshape(d, 3, B_KV_HEADS, B_GROUP).transpose(0, 2, 1, 3)
    wg = wg.reshape(d, B_KV_HEADS, 3 * B_GROUP)
    wg = jnp.pad(wg, ((0, 0), (0, 0), (0, LANES - 3 * B_GROUP))).reshape(d, B_KV_HEADS * LANES)
    z, zg = _rms_matmul(h, norm_g, w_in[:, :n_main].astype(BF16), wg.astype(BF16))

    sgu_b_full = jnp.broadcast_to(sgu_b[:, :, None], (A_GROUPS, A_CHUNK, HEAD_DIM))
    a_out = _mixer_a(z, ln_g, ln_b, sgu_w, sgu_b_full)

    gw = B_GROUP * HEAD_DIM
    q_col0 = 3 * A_WIDTH // gw
    bg_col0 = (3 * A_WIDTH + B_WIDTH) // gw
    kv_col0 = (3 * A_WIDTH + 2 * B_WIDTH) // HEAD_DIM
    w1 = cmp_w1.reshape(2, CMP_LEN, HEAD_DIM, CMP_HIDDEN).astype(BF16)
    kc, vc, ks, vs, kw, vw = _kv_prep(z, batch, seq, kv_col0, k_gain, cmp_pos, w1,
                                      cmp_b1.reshape(2, 1, CMP_HIDDEN), cmp_w2.astype(BF16))
    b_out = _nsa_attn(z, zg, kc, vc, ks, vs, kw, vw, q_gain, batch, seq, q_col0, bg_col0)

    w_out = w_out.astype(BF16)
    return _matmul_residual(h, [a_out, b_out], [w_out[:A_WIDTH], w_out[A_WIDTH:]])


def _odd_layer(h, batch, seq, layer, norm_g, w_in, lb_logits, out_gain, w_out):
    z = _rms_matmul(h, norm_g, w_in.astype(BF16))
    o = _hgrn(z, lb_logits, out_gain, batch, seq, layer)
    return _matmul_residual(h, [o], [w_out.astype(BF16)])


def kernel(x, even_norm, even_w_in, sgu_ln_g, sgu_ln_b, sgu_w, sgu_b, nsa_q_gain, nsa_k_gain, cmp_pos, cmp_w1, cmp_b1, cmp_w2, even_w_out, odd_norm, odd_w_in, hgrn_lb, hgrn_out_gain, odd_w_out):
    batch, seq, d = x.shape
    depth = hgrn_lb.shape[0]
    h = x.reshape(batch * seq, d)
    for layer in range(depth):
        if layer % 2 == 0:
            e = layer // 2
            h = _even_layer(h, batch, seq, even_norm[e], even_w_in[e], sgu_ln_g[e], sgu_ln_b[e],
                            sgu_w[e], sgu_b[e], nsa_q_gain[e], nsa_k_gain[e], cmp_pos[e], cmp_w1[e],
                            cmp_b1[e], cmp_w2[e], even_w_out[e])
        else:
            o = layer // 2
            h = _odd_layer(h, batch, seq, layer, odd_norm[o], odd_w_in[o], hgrn_lb, hgrn_out_gain[o],
                           odd_w_out[o])
    return h.reshape(batch, seq, d)
```

```python
import functools

import numpy as np
import jax
import jax.numpy as jnp
from jax import lax
from jax.experimental import pallas as pl
from jax.experimental.pallas import tpu as pltpu

HEAD_DIM = 128
A_GROUPS = 8
A_WIDTH = A_GROUPS * HEAD_DIM
A_CHUNK = 128
B_HEADS = 8
B_KV_HEADS = 2
B_GROUP = B_HEADS // B_KV_HEADS
B_WIDTH = B_HEADS * HEAD_DIM
CMP_LEN = 32
CMP_STRIDE = 16
CMP_HIDDEN = 256
SLC_BLOCK = 64
TOP_N = 8
WINDOW = 512
C_HEADS = 16
C_DIM = 128
C_CHUNK = 64
EPS = 1e-6
NEG_INF = -1e30
FORCE_SCORE = 1e6

LANES = 128
BF16_SUBLANES = 16
VMEM_LIMIT_BYTES = 56 * 1024 * 1024

PROJ_TM = 1024
PROJ_TN = 512
NORM_ROWS = 256
MIXA_ROWS = 512
ATT_TQ = 256
VT_ROWS = HEAD_DIM + BF16_SUBLANES
HGRN_HEADS = 8
HGRN_ROWS = 512

BF16 = jnp.bfloat16
F32 = jnp.float32


def _params(*sem):
    return pltpu.CompilerParams(dimension_semantics=sem, vmem_limit_bytes=VMEM_LIMIT_BYTES)


def _dot(a, b):
    return jnp.dot(a, b, preferred_element_type=F32)


def _dot_nt(a, b):
    return lax.dot_general(a, b, (((1,), (1,)), ((), ())), preferred_element_type=F32)


def _dot_tn(a, b):
    return lax.dot_general(a, b, (((0,), (0,)), ((), ())), preferred_element_type=F32)


def _rms(x, gain):
    return x * lax.rsqrt(jnp.mean(x * x, axis=-1, keepdims=True) + EPS) * gain


def _rms_matmul_kernel(x_ref, g_ref, w_ref, *rest, has_extra):
    if has_extra:
        w2_ref, o_ref, o2_ref, xn_ref = rest
    else:
        o_ref, xn_ref = rest

    @pl.when(pl.program_id(1) == 0)
    def _():
        def body(r, carry):
            r0 = pl.multiple_of(r * NORM_ROWS, NORM_ROWS)
            x = x_ref[pl.ds(r0, NORM_ROWS), :]
            xn_ref[pl.ds(r0, NORM_ROWS), :] = _rms(x, g_ref[...]).astype(BF16)
            return carry

        lax.fori_loop(0, x_ref.shape[0] // NORM_ROWS, body, 0)
        if has_extra:
            o2_ref[...] = _dot(xn_ref[...], w2_ref[...])

    o_ref[...] = _dot(xn_ref[...], w_ref[...])


def _rms_matmul(x, gain, w, w_extra=None):
    t, d = x.shape
    n = w.shape[1]
    tm = min(PROJ_TM, t)
    tn = min(PROJ_TN, n)
    assert t % tm == 0 and n % tn == 0 and tm % NORM_ROWS == 0
    in_specs = [
        pl.BlockSpec((tm, d), lambda i, j: (i, 0)),
        pl.BlockSpec((1, d), lambda i, j: (0, 0)),
        pl.BlockSpec((d, tn), lambda i, j: (0, j)),
    ]
    out_shape = [jax.ShapeDtypeStruct((t, n), F32)]
    out_specs = [pl.BlockSpec((tm, tn), lambda i, j: (i, j))]
    args = [x, gain.reshape(1, d), w]
    if w_extra is not None:
        n2 = w_extra.shape[1]
        in_specs.append(pl.BlockSpec((d, n2), lambda i, j: (0, 0)))
        out_shape.append(jax.ShapeDtypeStruct((t, n2), F32))
        out_specs.append(pl.BlockSpec((tm, n2), lambda i, j: (i, 0)))
        args.append(w_extra)
    out = pl.pallas_call(
        functools.partial(_rms_matmul_kernel, has_extra=w_extra is not None),
        grid=(t // tm, n // tn),
        in_specs=in_specs,
        out_specs=out_specs,
        out_shape=out_shape,
        scratch_shapes=[pltpu.VMEM((tm, d), BF16)],
        compiler_params=_params("parallel", "arbitrary"),
        name="rms_matmul",
    )(*args)
    return out if w_extra is not None else out[0]


def _matmul_residual_kernel(x_ref, *refs, n_pairs):
    acts, ws, o_ref = refs[:n_pairs], refs[n_pairs:2 * n_pairs], refs[2 * n_pairs]
    acc = x_ref[...]
    for a_ref, w_ref in zip(acts, ws):
        acc = acc + _dot(a_ref[...], w_ref[...])
    o_ref[...] = acc


def _matmul_residual(x, acts, ws):
    t, d = x.shape
    tm = min(PROJ_TM, t)
    tn = min(PROJ_TN, d)
    assert t % tm == 0 and d % tn == 0
    in_specs = [pl.BlockSpec((tm, tn), lambda i, j: (i, j))]
    in_specs += [pl.BlockSpec((tm, a.shape[1]), lambda i, j: (i, 0)) for a in acts]
    in_specs += [pl.BlockSpec((w.shape[0], tn), lambda i, j: (0, j)) for w in ws]
    return pl.pallas_call(
        functools.partial(_matmul_residual_kernel, n_pairs=len(acts)),
        grid=(t // tm, d // tn),
        in_specs=in_specs,
        out_specs=pl.BlockSpec((tm, tn), lambda i, j: (i, j)),
        out_shape=jax.ShapeDtypeStruct((t, d), F32),
        compiler_params=_params("parallel", "arbitrary"),
        name="matmul_residual",
    )(x, *acts, *ws)


def _mixer_a_kernel(u_ref, v_ref, gate_ref, lng_ref, lnb_ref, w_ref, b_ref, o_ref):
    rows = u_ref.shape[0]
    v = jax.nn.gelu(v_ref[...])
    mu = jnp.mean(v, axis=-1, keepdims=True)
    vc = v - mu
    var = jnp.mean(vc * vc, axis=-1, keepdims=True)
    vn = (vc * lax.rsqrt(var + EPS) * lng_ref[...] + lnb_ref[...]).astype(BF16)
    tr = lax.broadcasted_iota(jnp.int32, (A_CHUNK, A_CHUNK), 0)
    tc = lax.broadcasted_iota(jnp.int32, (A_CHUNK, A_CHUNK), 1)
    for g in range(A_GROUPS):
        cols = slice(g * HEAD_DIM, (g + 1) * HEAD_DIM)
        wm = jnp.where(tr >= tc, w_ref[g], 0.0).astype(BF16)
        for c in range(rows // A_CHUNK):
            rs = slice(c * A_CHUNK, (c + 1) * A_CHUNK)
            sv = _dot(wm, vn[rs, cols]) + b_ref[g]
            u = jax.nn.gelu(u_ref[rs, cols])
            o_ref[rs, cols] = (u * sv * jax.nn.silu(gate_ref[rs, cols])).astype(BF16)


def _mixer_a(z, ln_g, ln_b, sgu_w, sgu_b_full):
    t = z.shape[0]
    rows = min(MIXA_ROWS, t)
    assert t % rows == 0 and rows % A_CHUNK == 0
    blk = lambda c: pl.BlockSpec((rows, A_WIDTH), lambda i: (i, c))
    full = lambda a: pl.BlockSpec(a.shape, lambda i: (0,) * a.ndim)
    lng = ln_g.reshape(1, A_WIDTH)
    lnb = ln_b.reshape(1, A_WIDTH)
    return pl.pallas_call(
        _mixer_a_kernel,
        grid=(t // rows,),
        in_specs=[blk(0), blk(1), blk(2), full(lng), full(lnb), full(sgu_w), full(sgu_b_full)],
        out_specs=pl.BlockSpec((rows, A_WIDTH), lambda i: (i, 0)),
        out_shape=jax.ShapeDtypeStruct((t, A_WIDTH), BF16),
        compiler_params=_params("parallel"),
        name="mixer_a",
    )(z, z, z, lng, lnb, sgu_w, sgu_b_full)


def _compress(src_ref, pos_ref, w1_ref, b1_ref, w2_ref):
    groups = src_ref.shape[0] // CMP_STRIDE
    lo = jnp.zeros((groups, CMP_HIDDEN), F32)
    hi = jnp.zeros((groups, CMP_HIDDEN), F32)
    for l in range(CMP_STRIDE):
        xl = src_ref[pl.ds(l, groups, stride=CMP_STRIDE), :]
        lo = lo + _dot((xl + pos_ref[l:l + 1, :]).astype(BF16), w1_ref[l])
        hi = hi + _dot((xl + pos_ref[CMP_STRIDE + l:CMP_STRIDE + l + 1, :]).astype(BF16),
                       w1_ref[CMP_STRIDE + l])
    hid = jax.nn.gelu(lo + pltpu.roll(hi, groups - 1, 0) + b1_ref[...])
    return _dot(hid.astype(BF16), w2_ref[...])


def _store_values_t(v_src, vt_ref):
    n_tiles, rows, tk = vt_ref.shape
    for kt in range(n_tiles):
        vt_ref[kt, :HEAD_DIM, :] = v_src[kt * tk:(kt + 1) * tk, :].T.astype(BF16)
        vt_ref[kt, HEAD_DIM:, :] = jnp.ones((rows - HEAD_DIM, tk), BF16)


def _kv_prep_kernel(kc_src, vc_src, ks_src, vs_src, kw_src, vw_src, kgain_ref, pos_ref, w1_ref,
                    b1_ref, w2_ref, kc_ref, vct_ref, ks_ref, vst_ref, kw_ref, vwt_ref):
    ks_ref[...] = _rms(ks_src[...], kgain_ref[1:2, :]).astype(BF16)
    kw_ref[...] = _rms(kw_src[...], kgain_ref[2:3, :]).astype(BF16)
    _store_values_t(vs_src, vst_ref)
    _store_values_t(vw_src, vwt_ref)
    kc = _compress(kc_src, pos_ref.at[0], w1_ref.at[0], b1_ref.at[0], w2_ref.at[0])
    kc_ref[...] = _rms(kc, kgain_ref[0:1, :]).astype(BF16)
    vc = _compress(vc_src, pos_ref.at[1], w1_ref.at[1], b1_ref.at[1], w2_ref.at[1])
    vct_ref[...] = vc.T.astype(BF16)


def _kv_prep(z, batch, seq, kv_col0, k_gain, cmp_pos, w1, b1, w2):
    groups = seq // CMP_STRIDE
    tk = min(ATT_TQ, seq)
    src = lambda i: pl.BlockSpec((seq, HEAD_DIM), lambda b, h, i=i: (b, kv_col0 + i * B_KV_HEADS + h))
    full = lambda a: pl.BlockSpec(a.shape, lambda b, h: (0,) * a.ndim)
    spec4 = lambda r, c: pl.BlockSpec((None, None, r, c), lambda b, h: (b, h, 0, 0))
    shape4 = lambda r, c: jax.ShapeDtypeStruct((batch, B_KV_HEADS, r, c), BF16)
    vt_spec = pl.BlockSpec((None, None, seq // tk, VT_ROWS, tk), lambda b, h: (b, h, 0, 0, 0))
    vt_shape = jax.ShapeDtypeStruct((batch, B_KV_HEADS, seq // tk, VT_ROWS, tk), BF16)
    return pl.pallas_call(
        _kv_prep_kernel,
        grid=(batch, B_KV_HEADS),
        in_specs=[src(i) for i in range(6)] + [full(k_gain), full(cmp_pos), full(w1), full(b1), full(w2)],
        out_specs=[spec4(groups, HEAD_DIM), spec4(HEAD_DIM, groups), spec4(seq, HEAD_DIM), vt_spec,
                   spec4(seq, HEAD_DIM), vt_spec],
        out_shape=[shape4(groups, HEAD_DIM), shape4(HEAD_DIM, groups), shape4(seq, HEAD_DIM), vt_shape,
                   shape4(seq, HEAD_DIM), vt_shape],
        compiler_params=_params("parallel", "parallel"),
        name="kv_prep",
    )(z, z, z, z, z, z, k_gain, cmp_pos, w1, b1, w2)


def _nsa_kernel(slopes_ref, q_ref, bg_ref, gl_ref, kc_ref, vct_ref, ks_ref, vst_ref, kw_ref, vwt_ref,
                qgain_ref, ovt_ref, o_ref, qt_ref, m_ref, acc_ref, *, n_cmp):
    tq = q_ref.shape[0]
    n_grp = kc_ref.shape[0]
    n_slc = ovt_ref.shape[0]
    h = pl.program_id(1)
    i = pl.program_id(2)
    t0 = i * tq

    for g in range(B_GROUP):
        q = q_ref[:, g * HEAD_DIM:(g + 1) * HEAD_DIM]
        qt_ref[g] = (_rms(q, qgain_ref[...]) * (HEAD_DIM ** -0.5)).T.astype(BF16)

    n_idx = lax.broadcasted_iota(jnp.int32, (n_grp, tq), 0)
    t_idx = t0 + lax.broadcasted_iota(jnp.int32, (n_grp, tq), 1)
    dist_c = t_idx - (n_idx * CMP_STRIDE + (CMP_LEN - 1))
    valid_c = (n_idx < n_cmp) & (dist_c >= 0)
    dist_cf = dist_c.astype(F32)
    p_sum = jnp.zeros((n_grp, tq), F32)
    o_cmp = []
    for g in range(B_GROUP):
        slope = slopes_ref[h * B_GROUP + g]
        s = _dot(kc_ref[...], qt_ref[g]) - slope * dist_cf
        s = jnp.where(valid_c, s, NEG_INF)
        m = jnp.max(s, axis=0, keepdims=True)
        e = jnp.where(valid_c, jnp.exp(s - m), 0.0)
        p = e / jnp.maximum(jnp.sum(e, axis=0, keepdims=True), 1e-30)
        p_sum = p_sum + p
        o_cmp.append(_dot(vct_ref[...], p.astype(BF16)))

    imp = jnp.dot(ovt_ref[...], p_sum, precision=lax.Precision.HIGHEST, preferred_element_type=F32)
    j_idx = lax.broadcasted_iota(jnp.int32, (n_slc, tq), 0)
    j_f = j_idx.astype(F32)
    t_sel = t0 + lax.broadcasted_iota(jnp.int32, (n_slc, tq), 1)
    cur = t_sel // SLC_BLOCK
    forced = (j_idx == 0) | (j_idx == cur) | (j_idx == cur - 1)
    val = jnp.where(forced, FORCE_SCORE, jnp.where(j_idx * SLC_BLOCK <= t_sel, imp, -1.0))
    sel_t = jnp.zeros((n_slc, tq), F32)
    for _ in range(min(TOP_N, n_slc)):
        best = jnp.max(val, axis=0, keepdims=True)
        first = jnp.min(jnp.where(val == best, j_f, float(n_slc)), axis=0, keepdims=True)
        hit = j_f == first
        sel_t = jnp.where(hit, 1.0, sel_t)
        val = jnp.where(hit, -jnp.inf, val)
    sel_pad = jnp.concatenate([sel_t, jnp.zeros((LANES - n_slc, tq), F32)], axis=0).astype(BF16)

    key_r = lax.broadcasted_iota(jnp.int32, (tq, tq), 0)
    qry_c = lax.broadcasted_iota(jnp.int32, (tq, tq), 1)
    causal = key_r <= qry_c
    rel_diag = key_r - qry_c
    blk_r = lax.broadcasted_iota(jnp.int32, (tq, LANES), 0)
    blk_j = lax.broadcasted_iota(jnp.int32, (tq, LANES), 1)

    def sel_mask(kt):
        expand_t = jnp.where((kt * tq + blk_r) // SLC_BLOCK == blk_j, 1.0, 0.0).astype(BF16)
        return _dot(expand_t, sel_pad) > 0.5

    def run_branch(k_ref, vt_ref, diag_mask, lo_tile, off_mask_fn):
        k_blk = k_ref[pl.ds(pl.multiple_of(t0, tq), tq), :]
        vt_blk = vt_ref[i]
        rel_f = rel_diag.astype(F32)
        for g in range(B_GROUP):
            slope = slopes_ref[h * B_GROUP + g]
            s = _dot(k_blk, qt_ref[g]) + slope * rel_f
            s = jnp.where(diag_mask, s, NEG_INF)
            m = jnp.max(s, axis=0, keepdims=True)
            m_ref[g] = m
            acc_ref[g] = _dot(vt_blk, jnp.exp(s - m).astype(BF16))

        def body(kt, carry):
            k0 = pl.multiple_of(kt * tq, tq)
            k_blk = k_ref[pl.ds(k0, tq), :]
            vt_blk = vt_ref[kt]
            mask = off_mask_fn(kt)
            rel = (rel_diag + (k0 - t0)).astype(F32)
            for g in range(B_GROUP):
                slope = slopes_ref[h * B_GROUP + g]
                s = _dot(k_blk, qt_ref[g]) + slope * rel
                s = jnp.where(mask, s, NEG_INF)
                m_old = m_ref[g]
                m_new = jnp.maximum(m_old, jnp.max(s, axis=0, keepdims=True))
                p = jnp.exp(s - m_new).astype(BF16)
                acc_ref[g] = jnp.exp(m_old - m_new) * acc_ref[g] + _dot(vt_blk, p)
                m_ref[g] = m_new
            return carry

        lax.fori_loop(lo_tile, i, body, 0)
        return [acc_ref[g, :HEAD_DIM, :] / acc_ref[g, HEAD_DIM:HEAD_DIM + 1, :] for g in range(B_GROUP)]

    o_slc = run_branch(ks_ref, vst_ref, causal & sel_mask(i), 0, sel_mask)

    def win_mask(kt):
        return (qry_c - key_r) + (i - kt) * tq < WINDOW

    win_tiles = (WINDOW + tq - 1) // tq
    o_win = run_branch(kw_ref, vwt_ref, causal, jnp.maximum(i - win_tiles, 0), win_mask)

    gates_t = jax.nn.sigmoid(gl_ref[...].T)
    for g in range(B_GROUP):
        c = slice(g * HEAD_DIM, (g + 1) * HEAD_DIM)
        o_t = (gates_t[g:g + 1] * o_cmp[g]
               + gates_t[B_GROUP + g:B_GROUP + g + 1] * o_slc[g]
               + gates_t[2 * B_GROUP + g:2 * B_GROUP + g + 1] * o_win[g])
        o_ref[:, c] = (o_t.T * jax.nn.silu(bg_ref[:, c])).astype(BF16)


def _overlap_t(n_grp, n_cmp, n_slc):
    cs = np.arange(n_grp)[None, :] * CMP_STRIDE
    ss = np.arange(n_slc)[:, None] * SLC_BLOCK
    ov = np.clip(np.minimum(cs + CMP_LEN, ss + SLC_BLOCK) - np.maximum(cs, ss), 0, None) / CMP_LEN
    ov = ov * (np.arange(n_grp)[None, :] < n_cmp)
    return jnp.asarray(ov, dtype=F32)


def _nsa_attn(z, zg, kc, vct, ks, vst, kw, vwt, q_gain, batch, seq, q_col0, bg_col0):
    t = z.shape[0]
    tq = min(ATT_TQ, seq)
    assert seq % tq == 0 and tq % SLC_BLOCK == 0 and seq // SLC_BLOCK <= LANES
    nq = seq // tq
    n_grp = seq // CMP_STRIDE
    n_cmp = (seq - CMP_LEN) // CMP_STRIDE + 1
    n_slc = seq // SLC_BLOCK
    gw = B_GROUP * HEAD_DIM
    h_idx = jnp.arange(1, B_HEADS + 1, dtype=F32)
    slopes = jnp.exp2(-8.0 * h_idx / B_HEADS)
    ovt = _overlap_t(n_grp, n_cmp, n_slc)
    rows = lambda b, h, i: b * nq + i
    spec4 = lambda r, c: pl.BlockSpec((None, None, r, c), lambda b, h, i: (b, h, 0, 0))
    vt_spec = pl.BlockSpec((None, None, nq, VT_ROWS, tq), lambda b, h, i: (b, h, 0, 0, 0))
    full = lambda a: pl.BlockSpec(a.shape, lambda b, h, i: (0,) * a.ndim)
    qg = q_gain.reshape(1, HEAD_DIM)
    return pl.pallas_call(
        functools.partial(_nsa_kernel, n_cmp=n_cmp),
        grid=(batch, B_KV_HEADS, nq),
        in_specs=[
            pl.BlockSpec(memory_space=pltpu.SMEM),
            pl.BlockSpec((tq, gw), lambda b, h, i: (rows(b, h, i), q_col0 + h)),
            pl.BlockSpec((tq, gw), lambda b, h, i: (rows(b, h, i), bg_col0 + h)),
            pl.BlockSpec((tq, LANES), lambda b, h, i: (rows(b, h, i), h)),
            spec4(n_grp, HEAD_DIM), spec4(HEAD_DIM, n_grp), spec4(seq, HEAD_DIM), vt_spec,
            spec4(seq, HEAD_DIM), vt_spec, full(qg), full(ovt),
        ],
        out_specs=pl.BlockSpec((tq, gw), lambda b, h, i: (rows(b, h, i), h)),
        out_shape=jax.ShapeDtypeStruct((t, B_WIDTH), BF16),
        scratch_shapes=[
            pltpu.VMEM((B_GROUP, HEAD_DIM, tq), BF16),
            pltpu.VMEM((B_GROUP, 1, tq), F32),
            pltpu.VMEM((B_GROUP, VT_ROWS, tq), F32),
        ],
        compiler_params=_params("parallel", "parallel", "arbitrary"),
        name="nsa_attn",
    )(slopes, z, z, zg, kc, vct, ks, vst, kw, vwt, qg, ovt)


def _hgrn_kernel(q_ref, f_ref, i_ref, gate_ref, lb_ref, og_ref, o_ref, state_ref, *, layer):
    n_rows = q_ref.shape[0]
    n_heads = q_ref.shape[1] // C_DIM
    mid = C_CHUNK // 2 - 1

    logits = lb_ref[...]
    e = jnp.exp(logits - jnp.max(logits, axis=0, keepdims=True))
    p = e / jnp.sum(e, axis=0, keepdims=True)
    lb = jnp.sum(p[:layer + 1], axis=0, keepdims=True) - p[0:1]

    r_i = lax.broadcasted_iota(jnp.int32, (C_CHUNK, C_CHUNK), 0)
    c_i = lax.broadcasted_iota(jnp.int32, (C_CHUNK, C_CHUNK), 1)
    lower = r_i >= c_i
    tri = jnp.where(lower, 1.0, 0.0)

    @pl.when(pl.program_id(2) == 0)
    def _():
        state_ref[...] = jnp.zeros_like(state_ref)

    def body(c, carry):
        r0 = pl.multiple_of(c * C_CHUNK, C_CHUNK)
        rows = pl.ds(r0, C_CHUNK)
        f = lb + (1.0 - lb) * jax.nn.sigmoid(f_ref[rows, :])
        k = 1.0 - f
        gcum = jnp.dot(tri, jnp.log(f), precision=lax.Precision.HIGHEST, preferred_element_type=F32)
        g_mid = gcum[mid:mid + 1, :]
        g_end = gcum[C_CHUNK - 1:C_CHUNK, :]
        qa = q_ref[rows, :] * jnp.exp(gcum - g_mid)
        kb = k * jnp.exp(g_mid - gcum)
        q_in = (qa * jnp.exp(g_mid)).astype(BF16)
        k_st = (kb * jnp.exp(g_end - g_mid)).astype(BF16)
        qa = qa.astype(BF16)
        kb = kb.astype(BF16)
        v = i_ref[rows, :].astype(BF16)
        decay = jnp.exp(g_end)
        for hh in range(n_heads):
            cs = slice(hh * C_DIM, (hh + 1) * C_DIM)
            attn = jnp.where(lower, _dot_nt(qa[:, cs], kb[:, cs]), 0.0)
            state = state_ref[hh]
            o = _dot(attn.astype(BF16), v[:, cs]) + _dot(q_in[:, cs], state.astype(BF16))
            decay_rows = jnp.broadcast_to(decay[:, cs], (C_DIM, C_DIM)).T
            state_ref[hh] = decay_rows * state + _dot_tn(k_st[:, cs], v[:, cs])
            o = _rms(o, og_ref[...]) * jax.nn.silu(gate_ref[rows, cs])
            o_ref[rows, cs] = o.astype(BF16)
        return carry

    lax.fori_loop(0, n_rows // C_CHUNK, body, 0)


def _hgrn(z, lb_logits, out_gain, batch, seq, layer):
    t = z.shape[0]
    width = C_HEADS * C_DIM
    bw = HGRN_HEADS * C_DIM
    nb = width // bw
    rows = min(HGRN_ROWS, seq)
    ns = seq // rows
    assert seq % rows == 0 and rows % C_CHUNK == 0
    depth = lb_logits.shape[0]
    col = lambda part: pl.BlockSpec((rows, bw), lambda b, j, s, part=part: (b * ns + s, part * nb + j))
    og = out_gain.reshape(1, C_DIM)
    return pl.pallas_call(
        functools.partial(_hgrn_kernel, layer=layer),
        grid=(batch, nb, ns),
        in_specs=[col(0), col(1), col(2), col(3),
                  pl.BlockSpec((depth, bw), lambda b, j, s: (0, j)),
                  pl.BlockSpec((1, C_DIM), lambda b, j, s: (0, 0))],
        out_specs=pl.BlockSpec((rows, bw), lambda b, j, s: (b * ns + s, j)),
        out_shape=jax.ShapeDtypeStruct((t, width), BF16),
        scratch_shapes=[pltpu.VMEM((HGRN_HEADS, C_DIM, C_DIM), F32)],
        compiler_params=_params("parallel", "parallel", "arbitrary"),
        name="hgrn",
    )(z, z, z, z, lb_logits, og)


def _even_layer(h, batch, seq, norm_g, w_in, ln_g, ln_b, sgu_w, sgu_b, q_gain, k_gain,
                cmp_pos, cmp_w1, cmp_b1, cmp_w2, w_out):
    d = h.shape[1]
    n_main = 3 * A_WIDTH + 2 * B_WIDTH + 6 * B_KV_HEADS * HEAD_DIM
    wg = w_in[:, n_main:].reshape(d, 3, B_KV_HEADS, B_GROUP).transpose(0, 2, 1, 3)
    wg = wg.reshape(d, B_KV_HEADS, 3 * B_GROUP)
    wg = jnp.pad(wg, ((0, 0), (0, 0), (0, LANES - 3 * B_GROUP))).reshape(d, B_KV_HEADS * LANES)
    z, zg = _rms_matmul(h, norm_g, w_in[:, :n_main].astype(BF16), wg.astype(BF16))

    sgu_b_full = jnp.broadcast_to(sgu_b[:, :, None], (A_GROUPS, A_CHUNK, HEAD_DIM))
    a_out = _mixer_a(z, ln_g, ln_b, sgu_w, sgu_b_full)

    gw = B_GROUP * HEAD_DIM
    q_col0 = 3 * A_WIDTH // gw
    bg_col0 = (3 * A_WIDTH + B_WIDTH) // gw
    kv_col0 = (3 * A_WIDTH + 2 * B_WIDTH) // HEAD_DIM
    w1 = cmp_w1.reshape(2, CMP_LEN, HEAD_DIM, CMP_HIDDEN).astype(BF16)
    kc, vct, ks, vst, kw, vwt = _kv_prep(z, batch, seq, kv_col0, k_gain, cmp_pos, w1,
                                         cmp_b1.reshape(2, 1, CMP_HIDDEN), cmp_w2.astype(BF16))
    b_out = _nsa_attn(z, zg, kc, vct, ks, vst, kw, vwt, q_gain, batch, seq, q_col0, bg_col0)

    w_out = w_out.astype(BF16)
    return _matmul_residual(h, [a_out, b_out], [w_out[:A_WIDTH], w_out[A_WIDTH:]])


def _odd_layer(h, batch, seq, layer, norm_g, w_in, lb_logits, out_gain, w_out):
    z = _rms_matmul(h, norm_g, w_in.astype(BF16))
    o = _hgrn(z, lb_logits, out_gain, batch, seq, layer)
    return _matmul_residual(h, [o], [w_out.astype(BF16)])


def kernel(x, even_norm, even_w_in, sgu_ln_g, sgu_ln_b, sgu_w, sgu_b, nsa_q_gain, nsa_k_gain, cmp_pos, cmp_w1, cmp_b1, cmp_w2, even_w_out, odd_norm, odd_w_in, hgrn_lb, hgrn_out_gain, odd_w_out):
    batch, seq, d = x.shape
    depth = hgrn_lb.shape[0]
    h = x.reshape(batch * seq, d)
    for layer in range(depth):
        if layer % 2 == 0:
            e = layer // 2
            h = _even_layer(h, batch, seq, even_norm[e], even_w_in[e], sgu_ln_g[e], sgu_ln_b[e],
                            sgu_w[e], sgu_b[e], nsa_q_gain[e], nsa_k_gain[e], cmp_pos[e], cmp_w1[e],
                            cmp_b1[e], cmp_w2[e], even_w_out[e])
        else:
            o = layer // 2
            h = _odd_layer(h, batch, seq, layer, odd_norm[o], odd_w_in[o], hgrn_lb, hgrn_out_gain[o],
                           odd_w_out[o])
    return h.reshape(batch, seq, d)
```

```python
import functools

import numpy as np
import jax
import jax.numpy as jnp
from jax import lax
from jax.experimental import pallas as pl
from jax.experimental.pallas import tpu as pltpu

HEAD_DIM = 128
A_GROUPS = 8
A_WIDTH = A_GROUPS * HEAD_DIM
A_CHUNK = 128
B_HEADS = 8
B_KV_HEADS = 2
B_GROUP = B_HEADS // B_KV_HEADS
B_WIDTH = B_HEADS * HEAD_DIM
CMP_LEN = 32
CMP_STRIDE = 16
CMP_HIDDEN = 256
SLC_BLOCK = 64
TOP_N = 8
WINDOW = 512
C_HEADS = 16
C_DIM = 128
C_CHUNK = 64
EPS = 1e-6
NEG_INF = -1e30
FORCE_SCORE = 1e6

LANES = 128
BF16_SUBLANES = 16
VMEM_LIMIT_BYTES = 56 * 1024 * 1024

PROJ_TM = 1024
PROJ_TN = 1024
NORM_ROWS = 256
MIXA_ROWS = 512
ATT_TQ = 256
VT_ROWS = HEAD_DIM + BF16_SUBLANES
KEY_COLS = 2 * HEAD_DIM
ALIBI_SPLIT = 64
EXTRA_BLOCK0 = BF16_SUBLANES
HGRN_HEADS = 8
HGRN_ROWS = 512

BF16 = jnp.bfloat16
F32 = jnp.float32


def _params(*sem):
    return pltpu.CompilerParams(dimension_semantics=sem, vmem_limit_bytes=VMEM_LIMIT_BYTES)


def _col_tile(n):
    if n <= PROJ_TN:
        return n
    return max(c for c in range(LANES, PROJ_TN + 1, LANES) if n % c == 0)


def _dot(a, b):
    return jnp.dot(a, b, preferred_element_type=F32)


def _dot_nt(a, b):
    return lax.dot_general(a, b, (((1,), (1,)), ((), ())), preferred_element_type=F32)


def _dot_tn(a, b):
    return lax.dot_general(a, b, (((0,), (0,)), ((), ())), preferred_element_type=F32)


def _rms(x, gain):
    return x * lax.rsqrt(jnp.mean(x * x, axis=-1, keepdims=True) + EPS) * gain


def _rms_matmul_kernel(x_ref, g_ref, w_ref, *rest, has_extra):
    if has_extra:
        w2_ref, o_ref, o2_ref, xn_ref = rest
    else:
        o_ref, xn_ref = rest

    @pl.when(pl.program_id(1) == 0)
    def _():
        def body(r, carry):
            r0 = pl.multiple_of(r * NORM_ROWS, NORM_ROWS)
            x = x_ref[pl.ds(r0, NORM_ROWS), :]
            xn_ref[pl.ds(r0, NORM_ROWS), :] = _rms(x, g_ref[...]).astype(BF16)
            return carry

        lax.fori_loop(0, x_ref.shape[0] // NORM_ROWS, body, 0)
        if has_extra:
            o2_ref[...] = _dot(xn_ref[...], w2_ref[...])

    o_ref[...] = _dot(xn_ref[...], w_ref[...])


def _rms_matmul(x, gain, w, w_extra=None):
    t, d = x.shape
    n = w.shape[1]
    tm = min(PROJ_TM, t)
    tn = _col_tile(n)
    assert t % tm == 0 and n % tn == 0 and tm % NORM_ROWS == 0
    in_specs = [
        pl.BlockSpec((tm, d), lambda i, j: (i, 0)),
        pl.BlockSpec((1, d), lambda i, j: (0, 0)),
        pl.BlockSpec((d, tn), lambda i, j: (0, j)),
    ]
    out_shape = [jax.ShapeDtypeStruct((t, n), F32)]
    out_specs = [pl.BlockSpec((tm, tn), lambda i, j: (i, j))]
    args = [x, gain.reshape(1, d), w]
    if w_extra is not None:
        n2 = w_extra.shape[1]
        in_specs.append(pl.BlockSpec((d, n2), lambda i, j: (0, 0)))
        out_shape.append(jax.ShapeDtypeStruct((t, n2), F32))
        out_specs.append(pl.BlockSpec((tm, n2), lambda i, j: (i, 0)))
        args.append(w_extra)
    out = pl.pallas_call(
        functools.partial(_rms_matmul_kernel, has_extra=w_extra is not None),
        grid=(t // tm, n // tn),
        in_specs=in_specs,
        out_specs=out_specs,
        out_shape=out_shape,
        scratch_shapes=[pltpu.VMEM((tm, d), BF16)],
        compiler_params=_params("parallel", "arbitrary"),
        name="rms_matmul",
    )(*args)
    return out if w_extra is not None else out[0]


def _matmul_residual_kernel(x_ref, *refs, n_pairs):
    acts, ws, o_ref = refs[:n_pairs], refs[n_pairs:2 * n_pairs], refs[2 * n_pairs]
    acc = x_ref[...]
    for a_ref, w_ref in zip(acts, ws):
        acc = acc + _dot(a_ref[...], w_ref[...])
    o_ref[...] = acc


def _matmul_residual(x, acts, ws):
    t, d = x.shape
    tm = min(PROJ_TM, t)
    tn = _col_tile(d)
    assert t % tm == 0 and d % tn == 0
    in_specs = [pl.BlockSpec((tm, tn), lambda i, j: (i, j))]
    in_specs += [pl.BlockSpec((tm, a.shape[1]), lambda i, j: (i, 0)) for a in acts]
    in_specs += [pl.BlockSpec((w.shape[0], tn), lambda i, j: (0, j)) for w in ws]
    return pl.pallas_call(
        functools.partial(_matmul_residual_kernel, n_pairs=len(acts)),
        grid=(t // tm, d // tn),
        in_specs=in_specs,
        out_specs=pl.BlockSpec((tm, tn), lambda i, j: (i, j)),
        out_shape=jax.ShapeDtypeStruct((t, d), F32),
        compiler_params=_params("parallel", "arbitrary"),
        name="matmul_residual",
    )(x, *acts, *ws)


def _mixer_a_kernel(u_ref, v_ref, gate_ref, lng_ref, lnb_ref, w_ref, b_ref, o_ref):
    rows = u_ref.shape[0]
    v = jax.nn.gelu(v_ref[...])
    mu = jnp.mean(v, axis=-1, keepdims=True)
    vc = v - mu
    var = jnp.mean(vc * vc, axis=-1, keepdims=True)
    vn = (vc * lax.rsqrt(var + EPS) * lng_ref[...] + lnb_ref[...]).astype(BF16)
    tr = lax.broadcasted_iota(jnp.int32, (A_CHUNK, A_CHUNK), 0)
    tc = lax.broadcasted_iota(jnp.int32, (A_CHUNK, A_CHUNK), 1)
    for g in range(A_GROUPS):
        cols = slice(g * HEAD_DIM, (g + 1) * HEAD_DIM)
        wm = jnp.where(tr >= tc, w_ref[g], 0.0).astype(BF16)
        for c in range(rows // A_CHUNK):
            rs = slice(c * A_CHUNK, (c + 1) * A_CHUNK)
            sv = _dot(wm, vn[rs, cols]) + b_ref[g]
            u = jax.nn.gelu(u_ref[rs, cols])
            o_ref[rs, cols] = (u * sv * jax.nn.silu(gate_ref[rs, cols])).astype(BF16)


def _mixer_a(z, ln_g, ln_b, sgu_w, sgu_b_full):
    t = z.shape[0]
    rows = min(MIXA_ROWS, t)
    assert t % rows == 0 and rows % A_CHUNK == 0
    blk = lambda c: pl.BlockSpec((rows, A_WIDTH), lambda i: (i, c))
    full = lambda a: pl.BlockSpec(a.shape, lambda i: (0,) * a.ndim)
    lng = ln_g.reshape(1, A_WIDTH)
    lnb = ln_b.reshape(1, A_WIDTH)
    return pl.pallas_call(
        _mixer_a_kernel,
        grid=(t // rows,),
        in_specs=[blk(0), blk(1), blk(2), full(lng), full(lnb), full(sgu_w), full(sgu_b_full)],
        out_specs=pl.BlockSpec((rows, A_WIDTH), lambda i: (i, 0)),
        out_shape=jax.ShapeDtypeStruct((t, A_WIDTH), BF16),
        compiler_params=_params("parallel"),
        name="mixer_a",
    )(z, z, z, lng, lnb, sgu_w, sgu_b_full)


def _compress(src_ref, pos_ref, w1_ref, b1_ref, w2_ref):
    groups = src_ref.shape[0] // CMP_STRIDE
    lo = jnp.zeros((groups, CMP_HIDDEN), F32)
    hi = jnp.zeros((groups, CMP_HIDDEN), F32)
    for l in range(CMP_STRIDE):
        xl = src_ref[pl.ds(l, groups, stride=CMP_STRIDE), :]
        lo = lo + _dot((xl + pos_ref[l:l + 1, :]).astype(BF16), w1_ref[l])
        hi = hi + _dot((xl + pos_ref[CMP_STRIDE + l:CMP_STRIDE + l + 1, :]).astype(BF16),
                       w1_ref[CMP_STRIDE + l])
    hid = jax.nn.gelu(lo + pltpu.roll(hi, groups - 1, 0) + b1_ref[...])
    return _dot(hid.astype(BF16), w2_ref[...])


def _store_values_t(v_src, vt_ref):
    n_tiles, rows, tk = vt_ref.shape
    for kt in range(n_tiles):
        vt_ref[kt, :HEAD_DIM, :] = v_src[kt * tk:(kt + 1) * tk, :].T.astype(BF16)
        vt_ref[kt, HEAD_DIM:, :] = jnp.ones((rows - HEAD_DIM, tk), BF16)


def _key_extras(seq, with_blocks):
    kpos = lax.broadcasted_iota(jnp.int32, (seq, LANES), 0)
    c = lax.broadcasted_iota(jnp.int32, (seq, LANES), 1)
    hi = (kpos // ALIBI_SPLIT).astype(F32)
    lo = (kpos % ALIBI_SPLIT).astype(F32)
    ex = jnp.where(c == 0, hi, jnp.where(c == 1, lo, 0.0))
    if with_blocks:
        ex = jnp.where(kpos // SLC_BLOCK == c - EXTRA_BLOCK0, 1.0, ex)
    return ex.astype(BF16)


def _kv_prep_kernel(kc_src, vc_src, ks_src, vs_src, kw_src, vw_src, kgain_ref, pos_ref, w1_ref,
                    b1_ref, w2_ref, kc_ref, vct_ref, ks_ref, vst_ref, kw_ref, vwt_ref):
    seq = ks_src.shape[0]
    ks_ref[:, :HEAD_DIM] = _rms(ks_src[...], kgain_ref[1:2, :]).astype(BF16)
    ks_ref[:, HEAD_DIM:] = _key_extras(seq, True)
    kw_ref[:, :HEAD_DIM] = _rms(kw_src[...], kgain_ref[2:3, :]).astype(BF16)
    kw_ref[:, HEAD_DIM:] = _key_extras(seq, False)
    _store_values_t(vs_src, vst_ref)
    _store_values_t(vw_src, vwt_ref)
    kc = _compress(kc_src, pos_ref.at[0], w1_ref.at[0], b1_ref.at[0], w2_ref.at[0])
    kc_ref[...] = _rms(kc, kgain_ref[0:1, :]).astype(BF16)
    vc = _compress(vc_src, pos_ref.at[1], w1_ref.at[1], b1_ref.at[1], w2_ref.at[1])
    vct_ref[...] = vc.T.astype(BF16)


def _kv_prep(z, batch, seq, kv_col0, k_gain, cmp_pos, w1, b1, w2):
    groups = seq // CMP_STRIDE
    tk = min(ATT_TQ, seq)
    src = lambda i: pl.BlockSpec((seq, HEAD_DIM), lambda b, h, i=i: (b, kv_col0 + i * B_KV_HEADS + h))
    full = lambda a: pl.BlockSpec(a.shape, lambda b, h: (0,) * a.ndim)
    spec4 = lambda r, c: pl.BlockSpec((None, None, r, c), lambda b, h: (b, h, 0, 0))
    shape4 = lambda r, c: jax.ShapeDtypeStruct((batch, B_KV_HEADS, r, c), BF16)
    vt_spec = pl.BlockSpec((None, None, seq // tk, VT_ROWS, tk), lambda b, h: (b, h, 0, 0, 0))
    vt_shape = jax.ShapeDtypeStruct((batch, B_KV_HEADS, seq // tk, VT_ROWS, tk), BF16)
    return pl.pallas_call(
        _kv_prep_kernel,
        grid=(batch, B_KV_HEADS),
        in_specs=[src(i) for i in range(6)] + [full(k_gain), full(cmp_pos), full(w1), full(b1), full(w2)],
        out_specs=[spec4(groups, HEAD_DIM), spec4(HEAD_DIM, groups), spec4(seq, KEY_COLS), vt_spec,
                   spec4(seq, KEY_COLS), vt_spec],
        out_shape=[shape4(groups, HEAD_DIM), shape4(HEAD_DIM, groups), shape4(seq, KEY_COLS), vt_shape,
                   shape4(seq, KEY_COLS), vt_shape],
        compiler_params=_params("parallel", "parallel"),
        name="kv_prep",
    )(z, z, z, z, z, z, k_gain, cmp_pos, w1, b1, w2)


def _nsa_kernel(slopes_ref, q_ref, bg_ref, gl_ref, kc_ref, vct_ref, ks_ref, vst_ref, kw_ref, vwt_ref,
                qgain_ref, ovt_ref, o_ref, qt_ref, m_ref, acc_ref, *, n_cmp):
    tq = q_ref.shape[0]
    n_grp = kc_ref.shape[0]
    n_slc = ovt_ref.shape[0]
    h = pl.program_id(1)
    i = pl.program_id(2)
    t0 = i * tq

    for g in range(B_GROUP):
        q = q_ref[:, g * HEAD_DIM:(g + 1) * HEAD_DIM]
        qt_ref[g, :HEAD_DIM, :] = (_rms(q, qgain_ref[...]) * (HEAD_DIM ** -0.5)).T.astype(BF16)

    n_idx = lax.broadcasted_iota(jnp.int32, (n_grp, tq), 0)
    t_idx = t0 + lax.broadcasted_iota(jnp.int32, (n_grp, tq), 1)
    dist_c = t_idx - (n_idx * CMP_STRIDE + (CMP_LEN - 1))
    valid_c = (n_idx < n_cmp) & (dist_c >= 0)
    dist_cf = dist_c.astype(F32)
    groups = range(B_GROUP)
    s_c = [_dot(kc_ref[...], qt_ref[g, :HEAD_DIM, :]) for g in groups]
    s_c = [jnp.where(valid_c, s - slopes_ref[h * B_GROUP + g] * dist_cf, NEG_INF) for g, s in zip(groups, s_c)]
    m_c = [jnp.max(s, axis=0, keepdims=True) for s in s_c]
    e_c = [jnp.where(valid_c, jnp.exp(s - m), 0.0) for s, m in zip(s_c, m_c)]
    p_c = [e / jnp.maximum(jnp.sum(e, axis=0, keepdims=True), 1e-30) for e in e_c]
    o_cmp = [_dot(vct_ref[...], p.astype(BF16)) for p in p_c]
    p_sum = sum(p_c[1:], p_c[0])

    imp = jnp.dot(ovt_ref[...], p_sum, precision=lax.Precision.HIGHEST, preferred_element_type=F32)
    j_idx = lax.broadcasted_iota(jnp.int32, (n_slc, tq), 0)
    j_f = j_idx.astype(F32)
    t_sel = t0 + lax.broadcasted_iota(jnp.int32, (n_slc, tq), 1)
    cur = t_sel // SLC_BLOCK
    forced = (j_idx == 0) | (j_idx == cur) | (j_idx == cur - 1)
    val = jnp.where(forced, FORCE_SCORE, jnp.where(j_idx * SLC_BLOCK <= t_sel, imp, -1.0))
    sel_t = jnp.zeros((n_slc, tq), F32)
    for _ in range(min(TOP_N, n_slc)):
        best = jnp.max(val, axis=0, keepdims=True)
        first = jnp.min(jnp.where(val == best, j_f, float(n_slc)), axis=0, keepdims=True)
        hit = j_f == first
        sel_t = jnp.where(hit, 1.0, sel_t)
        val = jnp.where(hit, -jnp.inf, val)

    unselected = (sel_t - 1.0) * (-NEG_INF)
    e_row = lax.broadcasted_iota(jnp.int32, (EXTRA_BLOCK0, tq), 0)
    pad = jnp.zeros((HEAD_DIM - EXTRA_BLOCK0 - n_slc, tq), F32)
    for g in range(B_GROUP):
        slope = slopes_ref[h * B_GROUP + g]
        alibi = jnp.where(e_row == 0, slope * ALIBI_SPLIT, jnp.where(e_row == 1, slope, 0.0))
        qt_ref[g, HEAD_DIM:, :] = jnp.concatenate([alibi, unselected, pad], axis=0).astype(BF16)

    key_r = lax.broadcasted_iota(jnp.int32, (tq, tq), 0)
    qry_c = lax.broadcasted_iota(jnp.int32, (tq, tq), 1)
    causal = key_r <= qry_c

    def tile_update(k_ref, vt_ref, kt, mask, first):
        k_blk = k_ref[pl.ds(pl.multiple_of(kt * tq, tq), tq), :]
        vt_blk = vt_ref[kt]
        s_all = [_dot(k_blk, qt_ref[g]) for g in range(B_GROUP)]
        if mask is not None:
            s_all = [jnp.where(mask, s, NEG_INF) for s in s_all]
        m_all = [jnp.max(s, axis=0, keepdims=True) for s in s_all]
        if not first:
            m_old = [m_ref[g] for g in range(B_GROUP)]
            m_all = [jnp.maximum(a, b) for a, b in zip(m_old, m_all)]
        p_all = [jnp.exp(s - m).astype(BF16) for s, m in zip(s_all, m_all)]
        pv_all = [_dot(vt_blk, p) for p in p_all]
        for g in range(B_GROUP):
            if first:
                acc_ref[g] = pv_all[g]
            else:
                acc_ref[g] = jnp.exp(m_old[g] - m_all[g]) * acc_ref[g] + pv_all[g]
            m_ref[g] = m_all[g]

    def finish():
        return [acc_ref[g, :HEAD_DIM, :] / acc_ref[g, HEAD_DIM:HEAD_DIM + 1, :] for g in range(B_GROUP)]

    tile_update(ks_ref, vst_ref, i, causal, True)

    def slc_body(kt, carry):
        tile_update(ks_ref, vst_ref, kt, None, False)
        return carry

    lax.fori_loop(0, i, slc_body, 0)
    o_slc = finish()

    tile_update(kw_ref, vwt_ref, i, causal, True)
    for off in range(1, (WINDOW + tq - 1) // tq + 1):
        needs_mask = off * tq + tq - 1 >= WINDOW
        mask = ((qry_c - key_r) + off * tq < WINDOW) if needs_mask else None

        @pl.when(i >= off)
        def _(off=off, mask=mask):
            tile_update(kw_ref, vwt_ref, i - off, mask, False)

    o_win = finish()

    gates_t = jax.nn.sigmoid(gl_ref[...].T)
    for g in range(B_GROUP):
        c = slice(g * HEAD_DIM, (g + 1) * HEAD_DIM)
        o_t = (gates_t[g:g + 1] * o_cmp[g]
               + gates_t[B_GROUP + g:B_GROUP + g + 1] * o_slc[g]
               + gates_t[2 * B_GROUP + g:2 * B_GROUP + g + 1] * o_win[g])
        o_ref[:, c] = (o_t.T * jax.nn.silu(bg_ref[:, c])).astype(BF16)


def _overlap_t(n_grp, n_cmp, n_slc):
    cs = np.arange(n_grp)[None, :] * CMP_STRIDE
    ss = np.arange(n_slc)[:, None] * SLC_BLOCK
    ov = np.clip(np.minimum(cs + CMP_LEN, ss + SLC_BLOCK) - np.maximum(cs, ss), 0, None) / CMP_LEN
    ov = ov * (np.arange(n_grp)[None, :] < n_cmp)
    return jnp.asarray(ov, dtype=F32)


def _nsa_attn(z, zg, kc, vct, ks, vst, kw, vwt, q_gain, batch, seq, q_col0, bg_col0):
    t = z.shape[0]
    tq = min(ATT_TQ, seq)
    assert seq % tq == 0 and tq % SLC_BLOCK == 0 and seq // SLC_BLOCK <= LANES
    nq = seq // tq
    n_grp = seq // CMP_STRIDE
    n_cmp = (seq - CMP_LEN) // CMP_STRIDE + 1
    n_slc = seq // SLC_BLOCK
    gw = B_GROUP * HEAD_DIM
    h_idx = jnp.arange(1, B_HEADS + 1, dtype=F32)
    slopes = jnp.exp2(-8.0 * h_idx / B_HEADS)
    ovt = _overlap_t(n_grp, n_cmp, n_slc)
    rows = lambda b, h, i: b * nq + i
    spec4 = lambda r, c: pl.BlockSpec((None, None, r, c), lambda b, h, i: (b, h, 0, 0))
    vt_spec = pl.BlockSpec((None, None, nq, VT_ROWS, tq), lambda b, h, i: (b, h, 0, 0, 0))
    full = lambda a: pl.BlockSpec(a.shape, lambda b, h, i: (0,) * a.ndim)
    qg = q_gain.reshape(1, HEAD_DIM)
    return pl.pallas_call(
        functools.partial(_nsa_kernel, n_cmp=n_cmp),
        grid=(batch, B_KV_HEADS, nq),
        in_specs=[
            pl.BlockSpec(memory_space=pltpu.SMEM),
            pl.BlockSpec((tq, gw), lambda b, h, i: (rows(b, h, i), q_col0 + h)),
            pl.BlockSpec((tq, gw), lambda b, h, i: (rows(b, h, i), bg_col0 + h)),
            pl.BlockSpec((tq, LANES), lambda b, h, i: (rows(b, h, i), h)),
            spec4(n_grp, HEAD_DIM), spec4(HEAD_DIM, n_grp), spec4(seq, KEY_COLS), vt_spec,
            spec4(seq, KEY_COLS), vt_spec, full(qg), full(ovt),
        ],
        out_specs=pl.BlockSpec((tq, gw), lambda b, h, i: (rows(b, h, i), h)),
        out_shape=jax.ShapeDtypeStruct((t, B_WIDTH), BF16),
        scratch_shapes=[
            pltpu.VMEM((B_GROUP, KEY_COLS, tq), BF16),
            pltpu.VMEM((B_GROUP, 1, tq), F32),
            pltpu.VMEM((B_GROUP, VT_ROWS, tq), F32),
        ],
        compiler_params=_params("parallel", "parallel", "arbitrary"),
        name="nsa_attn",
    )(slopes, z, z, zg, kc, vct, ks, vst, kw, vwt, qg, ovt)


def _hgrn_kernel(q_ref, f_ref, i_ref, gate_ref, lb_ref, og_ref, o_ref, state_ref, *, layer):
    n_rows = q_ref.shape[0]
    n_heads = q_ref.shape[1] // C_DIM
    mid = C_CHUNK // 2 - 1

    logits = lb_ref[...]
    e = jnp.exp(logits - jnp.max(logits, axis=0, keepdims=True))
    p = e / jnp.sum(e, axis=0, keepdims=True)
    lb = jnp.sum(p[:layer + 1], axis=0, keepdims=True) - p[0:1]

    r_i = lax.broadcasted_iota(jnp.int32, (C_CHUNK, C_CHUNK), 0)
    c_i = lax.broadcasted_iota(jnp.int32, (C_CHUNK, C_CHUNK), 1)
    lower = r_i >= c_i
    tri = jnp.where(lower, 1.0, 0.0)

    @pl.when(pl.program_id(2) == 0)
    def _():
        state_ref[...] = jnp.zeros_like(state_ref)

    def body(c, carry):
        r0 = pl.multiple_of(c * C_CHUNK, C_CHUNK)
        rows = pl.ds(r0, C_CHUNK)
        f = lb + (1.0 - lb) * jax.nn.sigmoid(f_ref[rows, :])
        k = 1.0 - f
        gcum = jnp.dot(tri, jnp.log(f), precision=lax.Precision.HIGHEST, preferred_element_type=F32)
        g_mid = gcum[mid:mid + 1, :]
        g_end = gcum[C_CHUNK - 1:C_CHUNK, :]
        qa = q_ref[rows, :] * jnp.exp(gcum - g_mid)
        kb = k * jnp.exp(g_mid - gcum)
        q_in = (qa * jnp.exp(g_mid)).astype(BF16)
        k_st = (kb * jnp.exp(g_end - g_mid)).astype(BF16)
        qa = qa.astype(BF16)
        kb = kb.astype(BF16)
        v = i_ref[rows, :].astype(BF16)
        decay = jnp.exp(g_end)
        heads = range(n_heads)
        cs = [slice(hh * C_DIM, (hh + 1) * C_DIM) for hh in heads]
        attn = [_dot_nt(qa[:, c], kb[:, c]) for c in cs]
        state = [state_ref[hh] for hh in heads]
        inter = [_dot(q_in[:, c], s.astype(BF16)) for c, s in zip(cs, state)]
        update = [_dot_tn(k_st[:, c], v[:, c]) for c in cs]
        attn = [jnp.where(lower, a, 0.0).astype(BF16) for a in attn]
        o = [_dot(a, v[:, c]) + x for a, c, x in zip(attn, cs, inter)]
        for hh in heads:
            decay_rows = jnp.broadcast_to(decay[:, cs[hh]], (C_DIM, C_DIM)).T
            state_ref[hh] = decay_rows * state[hh] + update[hh]
        for hh in heads:
            out = _rms(o[hh], og_ref[...]) * jax.nn.silu(gate_ref[rows, cs[hh]])
            o_ref[rows, cs[hh]] = out.astype(BF16)
        return carry

    lax.fori_loop(0, n_rows // C_CHUNK, body, 0, unroll=2)


def _hgrn(z, lb_logits, out_gain, batch, seq, layer):
    t = z.shape[0]
    width = C_HEADS * C_DIM
    bw = HGRN_HEADS * C_DIM
    nb = width // bw
    rows = min(HGRN_ROWS, seq)
    ns = seq // rows
    assert seq % rows == 0 and rows % C_CHUNK == 0
    depth = lb_logits.shape[0]
    col = lambda part: pl.BlockSpec((rows, bw), lambda b, j, s, part=part: (b * ns + s, part * nb + j))
    og = out_gain.reshape(1, C_DIM)
    return pl.pallas_call(
        functools.partial(_hgrn_kernel, layer=layer),
        grid=(batch, nb, ns),
        in_specs=[col(0), col(1), col(2), col(3),
                  pl.BlockSpec((depth, bw), lambda b, j, s: (0, j)),
                  pl.BlockSpec((1, C_DIM), lambda b, j, s: (0, 0))],
        out_specs=pl.BlockSpec((rows, bw), lambda b, j, s: (b * ns + s, j)),
        out_shape=jax.ShapeDtypeStruct((t, width), BF16),
        scratch_shapes=[pltpu.VMEM((HGRN_HEADS, C_DIM, C_DIM), F32)],
        compiler_params=_params("parallel", "parallel", "arbitrary"),
        name="hgrn",
    )(z, z, z, z, lb_logits, og)


def _even_layer(h, batch, seq, norm_g, w_in, ln_g, ln_b, sgu_w, sgu_b, q_gain, k_gain,
                cmp_pos, cmp_w1, cmp_b1, cmp_w2, w_out):
    d = h.shape[1]
    n_main = 3 * A_WIDTH + 2 * B_WIDTH + 6 * B_KV_HEADS * HEAD_DIM
    wg = w_in[:, n_main:].reshape(d, 3, B_KV_HEADS, B_GROUP).transpose(0, 2, 1, 3)
    wg = wg.reshape(d, B_KV_HEADS, 3 * B_GROUP)
    wg = jnp.pad(wg, ((0, 0), (0, 0), (0, LANES - 3 * B_GROUP))).reshape(d, B_KV_HEADS * LANES)
    z, zg = _rms_matmul(h, norm_g, w_in[:, :n_main].astype(BF16), wg.astype(BF16))

    sgu_b_full = jnp.broadcast_to(sgu_b[:, :, None], (A_GROUPS, A_CHUNK, HEAD_DIM))
    a_out = _mixer_a(z, ln_g, ln_b, sgu_w, sgu_b_full)

    gw = B_GROUP * HEAD_DIM
    q_col0 = 3 * A_WIDTH // gw
    bg_col0 = (3 * A_WIDTH + B_WIDTH) // gw
    kv_col0 = (3 * A_WIDTH + 2 * B_WIDTH) // HEAD_DIM
    w1 = cmp_w1.reshape(2, CMP_LEN, HEAD_DIM, CMP_HIDDEN).astype(BF16)
    kc, vct, ks, vst, kw, vwt = _kv_prep(z, batch, seq, kv_col0, k_gain, cmp_pos, w1,
                                         cmp_b1.reshape(2, 1, CMP_HIDDEN), cmp_w2.astype(BF16))
    b_out = _nsa_attn(z, zg, kc, vct, ks, vst, kw, vwt, q_gain, batch, seq, q_col0, bg_col0)

    w_out = w_out.astype(BF16)
    return _matmul_residual(h, [a_out, b_out], [w_out[:A_WIDTH], w_out[A_WIDTH:]])


def _odd_layer(h, batch, seq, layer, norm_g, w_in, lb_logits, out_gain, w_out):
    z = _rms_matmul(h, norm_g, w_in.astype(BF16))
    o = _hgrn(z, lb_logits, out_gain, batch, seq, layer)
    return _matmul_residual(h, [o], [w_out.astype(BF16)])


def kernel(x, even_norm, even_w_in, sgu_ln_g, sgu_ln_b, sgu_w, sgu_b, nsa_q_gain, nsa_k_gain, cmp_pos, cmp_w1, cmp_b1, cmp_w2, even_w_out, odd_norm, odd_w_in, hgrn_lb, hgrn_out_gain, odd_w_out):
    batch, seq, d = x.shape
    depth = hgrn_lb.shape[0]
    h = x.reshape(batch * seq, d)
    for layer in range(depth):
        if layer % 2 == 0:
            e = layer // 2
            h = _even_layer(h, batch, seq, even_norm[e], even_w_in[e], sgu_ln_g[e], sgu_ln_b[e],
                            sgu_w[e], sgu_b[e], nsa_q_gain[e], nsa_k_gain[e], cmp_pos[e], cmp_w1[e],
                            cmp_b1[e], cmp_w2[e], even_w_out[e])
        else:
            o = layer // 2
            h = _odd_layer(h, batch, seq, layer, odd_norm[o], odd_w_in[o], hgrn_lb, hgrn_out_gain[o],
                           odd_w_out[o])
    return h.reshape(batch, seq, d)
```

```python
import functools

import numpy as np
import jax
import jax.numpy as jnp
from jax import lax
from jax.experimental import pallas as pl
from jax.experimental.pallas import tpu as pltpu

HEAD_DIM = 128
A_GROUPS = 8
A_WIDTH = A_GROUPS * HEAD_DIM
A_CHUNK = 128
B_HEADS = 8
B_KV_HEADS = 2
B_GROUP = B_HEADS // B_KV_HEADS
B_WIDTH = B_HEADS * HEAD_DIM
CMP_LEN = 32
CMP_STRIDE = 16
CMP_HIDDEN = 256
SLC_BLOCK = 64
TOP_N = 8
WINDOW = 512
C_HEADS = 16
C_DIM = 128
C_CHUNK = 64
EPS = 1e-6
NEG_INF = -1e30
FORCE_SCORE = 1e6

LANES = 128
BF16_SUBLANES = 16
VMEM_LIMIT_BYTES = 56 * 1024 * 1024

PROJ_TM = 1024
PROJ_TN = 1024
RESID_TM = 512
RESID_TN = 2048
NORM_ROWS = 256
MIXA_ROWS = 512
ATT_TQ = 256
VT_ROWS = HEAD_DIM + BF16_SUBLANES
KEY_COLS = 2 * HEAD_DIM
ALIBI_SPLIT = 64
EXTRA_BLOCK0 = BF16_SUBLANES
HGRN_HEADS = 8
HGRN_ROWS = 512

BF16 = jnp.bfloat16
F32 = jnp.float32


def _params(*sem):
    return pltpu.CompilerParams(dimension_semantics=sem, vmem_limit_bytes=VMEM_LIMIT_BYTES)


def _col_tile(n, limit):
    if n <= limit:
        return n
    return max(c for c in range(LANES, limit + 1, LANES) if n % c == 0)


def _dot(a, b):
    return jnp.dot(a, b, preferred_element_type=F32)


def _dot_nt(a, b):
    return lax.dot_general(a, b, (((1,), (1,)), ((), ())), preferred_element_type=F32)


def _dot_tn(a, b):
    return lax.dot_general(a, b, (((0,), (0,)), ((), ())), preferred_element_type=F32)


def _rms(x, gain):
    return x * lax.rsqrt(jnp.mean(x * x, axis=-1, keepdims=True) + EPS) * gain


def _rms_matmul_kernel(x_ref, g_ref, w_ref, *rest, has_extra):
    if has_extra:
        w2_ref, o_ref, o2_ref, xn_ref = rest
    else:
        o_ref, xn_ref = rest

    @pl.when(pl.program_id(1) == 0)
    def _():
        def body(r, carry):
            r0 = pl.multiple_of(r * NORM_ROWS, NORM_ROWS)
            x = x_ref[pl.ds(r0, NORM_ROWS), :]
            xn_ref[pl.ds(r0, NORM_ROWS), :] = _rms(x, g_ref[...]).astype(BF16)
            return carry

        lax.fori_loop(0, x_ref.shape[0] // NORM_ROWS, body, 0)
        if has_extra:
            o2_ref[...] = _dot(xn_ref[...], w2_ref[...])

    o_ref[...] = _dot(xn_ref[...], w_ref[...])


def _rms_matmul(x, gain, w, w_extra=None):
    t, d = x.shape
    n = w.shape[1]
    tm = min(PROJ_TM, t)
    tn = _col_tile(n, PROJ_TN)
    assert t % tm == 0 and n % tn == 0 and tm % NORM_ROWS == 0
    in_specs = [
        pl.BlockSpec((tm, d), lambda i, j: (i, 0)),
        pl.BlockSpec((1, d), lambda i, j: (0, 0)),
        pl.BlockSpec((d, tn), lambda i, j: (0, j)),
    ]
    out_shape = [jax.ShapeDtypeStruct((t, n), F32)]
    out_specs = [pl.BlockSpec((tm, tn), lambda i, j: (i, j))]
    args = [x, gain.reshape(1, d), w]
    if w_extra is not None:
        n2 = w_extra.shape[1]
        in_specs.append(pl.BlockSpec((d, n2), lambda i, j: (0, 0)))
        out_shape.append(jax.ShapeDtypeStruct((t, n2), F32))
        out_specs.append(pl.BlockSpec((tm, n2), lambda i, j: (i, 0)))
        args.append(w_extra)
    out = pl.pallas_call(
        functools.partial(_rms_matmul_kernel, has_extra=w_extra is not None),
        grid=(t // tm, n // tn),
        in_specs=in_specs,
        out_specs=out_specs,
        out_shape=out_shape,
        scratch_shapes=[pltpu.VMEM((tm, d), BF16)],
        compiler_params=_params("parallel", "arbitrary"),
        name="rms_matmul",
    )(*args)
    return out if w_extra is not None else out[0]


def _matmul_residual_kernel(x_ref, *refs, n_pairs):
    acts, ws, o_ref = refs[:n_pairs], refs[n_pairs:2 * n_pairs], refs[2 * n_pairs]
    acc = x_ref[...]
    for a_ref, w_ref in zip(acts, ws):
        acc = acc + _dot(a_ref[...], w_ref[...])
    o_ref[...] = acc


def _matmul_residual(x, acts, ws):
    t, d = x.shape
    tm = min(RESID_TM, t)
    tn = _col_tile(d, RESID_TN)
    assert t % tm == 0 and d % tn == 0
    in_specs = [pl.BlockSpec((tm, tn), lambda i, j: (i, j))]
    in_specs += [pl.BlockSpec((tm, a.shape[1]), lambda i, j: (i, 0)) for a in acts]
    in_specs += [pl.BlockSpec((w.shape[0], tn), lambda i, j: (0, j)) for w in ws]
    return pl.pallas_call(
        functools.partial(_matmul_residual_kernel, n_pairs=len(acts)),
        grid=(t // tm, d // tn),
        in_specs=in_specs,
        out_specs=pl.BlockSpec((tm, tn), lambda i, j: (i, j)),
        out_shape=jax.ShapeDtypeStruct((t, d), F32),
        compiler_params=_params("parallel", "arbitrary"),
        name="matmul_residual",
    )(x, *acts, *ws)


def _mixer_a_kernel(u_ref, v_ref, gate_ref, lng_ref, lnb_ref, w_ref, b_ref, o_ref):
    rows = u_ref.shape[0]
    v = jax.nn.gelu(v_ref[...])
    mu = jnp.mean(v, axis=-1, keepdims=True)
    vc = v - mu
    var = jnp.mean(vc * vc, axis=-1, keepdims=True)
    vn = (vc * lax.rsqrt(var + EPS) * lng_ref[...] + lnb_ref[...]).astype(BF16)
    tr = lax.broadcasted_iota(jnp.int32, (A_CHUNK, A_CHUNK), 0)
    tc = lax.broadcasted_iota(jnp.int32, (A_CHUNK, A_CHUNK), 1)
    for g in range(A_GROUPS):
        cols = slice(g * HEAD_DIM, (g + 1) * HEAD_DIM)
        wm = jnp.where(tr >= tc, w_ref[g], 0.0).astype(BF16)
        for c in range(rows // A_CHUNK):
            rs = slice(c * A_CHUNK, (c + 1) * A_CHUNK)
            sv = _dot(wm, vn[rs, cols]) + b_ref[g]
            u = jax.nn.gelu(u_ref[rs, cols])
            o_ref[rs, cols] = (u * sv * jax.nn.silu(gate_ref[rs, cols])).astype(BF16)


def _mixer_a(z, ln_g, ln_b, sgu_w, sgu_b_full):
    t = z.shape[0]
    rows = min(MIXA_ROWS, t)
    assert t % rows == 0 and rows % A_CHUNK == 0
    blk = lambda c: pl.BlockSpec((rows, A_WIDTH), lambda i: (i, c))
    full = lambda a: pl.BlockSpec(a.shape, lambda i: (0,) * a.ndim)
    lng = ln_g.reshape(1, A_WIDTH)
    lnb = ln_b.reshape(1, A_WIDTH)
    return pl.pallas_call(
        _mixer_a_kernel,
        grid=(t // rows,),
        in_specs=[blk(0), blk(1), blk(2), full(lng), full(lnb), full(sgu_w), full(sgu_b_full)],
        out_specs=pl.BlockSpec((rows, A_WIDTH), lambda i: (i, 0)),
        out_shape=jax.ShapeDtypeStruct((t, A_WIDTH), BF16),
        compiler_params=_params("parallel"),
        name="mixer_a",
    )(z, z, z, lng, lnb, sgu_w, sgu_b_full)


def _compress(src_ref, pos_ref, w1_ref, b1_ref, w2_ref):
    groups = src_ref.shape[0] // CMP_STRIDE
    lo = jnp.zeros((groups, CMP_HIDDEN), F32)
    hi = jnp.zeros((groups, CMP_HIDDEN), F32)
    for l in range(CMP_STRIDE):
        xl = src_ref[pl.ds(l, groups, stride=CMP_STRIDE), :]
        lo = lo + _dot((xl + pos_ref[l:l + 1, :]).astype(BF16), w1_ref[l])
        hi = hi + _dot((xl + pos_ref[CMP_STRIDE + l:CMP_STRIDE + l + 1, :]).astype(BF16),
                       w1_ref[CMP_STRIDE + l])
    hid = jax.nn.gelu(lo + pltpu.roll(hi, groups - 1, 0) + b1_ref[...])
    return _dot(hid.astype(BF16), w2_ref[...])


def _store_values_t(v_src, vt_ref):
    n_tiles, rows, tk = vt_ref.shape
    for kt in range(n_tiles):
        vt_ref[kt, :HEAD_DIM, :] = v_src[kt * tk:(kt + 1) * tk, :].T.astype(BF16)
        vt_ref[kt, HEAD_DIM:, :] = jnp.ones((rows - HEAD_DIM, tk), BF16)


def _key_extras(seq, with_blocks):
    kpos = lax.broadcasted_iota(jnp.int32, (seq, LANES), 0)
    c = lax.broadcasted_iota(jnp.int32, (seq, LANES), 1)
    hi = (kpos // ALIBI_SPLIT).astype(F32)
    lo = (kpos % ALIBI_SPLIT).astype(F32)
    ex = jnp.where(c == 0, hi, jnp.where(c == 1, lo, 0.0))
    if with_blocks:
        ex = jnp.where(kpos // SLC_BLOCK == c - EXTRA_BLOCK0, 1.0, ex)
    return ex.astype(BF16)


def _kv_prep_kernel(kc_src, vc_src, ks_src, vs_src, kw_src, vw_src, kgain_ref, pos_ref, w1_ref,
                    b1_ref, w2_ref, kc_ref, vct_ref, ks_ref, vst_ref, kw_ref, vwt_ref):
    seq = ks_src.shape[0]
    ks_ref[:, :HEAD_DIM] = _rms(ks_src[...], kgain_ref[1:2, :]).astype(BF16)
    ks_ref[:, HEAD_DIM:] = _key_extras(seq, True)
    kw_ref[:, :HEAD_DIM] = _rms(kw_src[...], kgain_ref[2:3, :]).astype(BF16)
    kw_ref[:, HEAD_DIM:] = _key_extras(seq, False)
    _store_values_t(vs_src, vst_ref)
    _store_values_t(vw_src, vwt_ref)
    kc = _compress(kc_src, pos_ref.at[0], w1_ref.at[0], b1_ref.at[0], w2_ref.at[0])
    kc_ref[...] = _rms(kc, kgain_ref[0:1, :]).astype(BF16)
    vc = _compress(vc_src, pos_ref.at[1], w1_ref.at[1], b1_ref.at[1], w2_ref.at[1])
    vct_ref[...] = vc.T.astype(BF16)


def _kv_prep(kv_srcs, batch, seq, k_gain, cmp_pos, w1, b1, w2):
    groups = seq // CMP_STRIDE
    tk = min(ATT_TQ, seq)
    src = lambda i: pl.BlockSpec((seq, HEAD_DIM), lambda b, h, c0=kv_srcs[i][1]: (b, c0 + h))
    full = lambda a: pl.BlockSpec(a.shape, lambda b, h: (0,) * a.ndim)
    spec4 = lambda r, c: pl.BlockSpec((None, None, r, c), lambda b, h: (b, h, 0, 0))
    shape4 = lambda r, c: jax.ShapeDtypeStruct((batch, B_KV_HEADS, r, c), BF16)
    vt_spec = pl.BlockSpec((None, None, seq // tk, VT_ROWS, tk), lambda b, h: (b, h, 0, 0, 0))
    vt_shape = jax.ShapeDtypeStruct((batch, B_KV_HEADS, seq // tk, VT_ROWS, tk), BF16)
    return pl.pallas_call(
        _kv_prep_kernel,
        grid=(batch, B_KV_HEADS),
        in_specs=[src(i) for i in range(6)] + [full(k_gain), full(cmp_pos), full(w1), full(b1), full(w2)],
        out_specs=[spec4(groups, HEAD_DIM), spec4(HEAD_DIM, groups), spec4(seq, KEY_COLS), vt_spec,
                   spec4(seq, KEY_COLS), vt_spec],
        out_shape=[shape4(groups, HEAD_DIM), shape4(HEAD_DIM, groups), shape4(seq, KEY_COLS), vt_shape,
                   shape4(seq, KEY_COLS), vt_shape],
        compiler_params=_params("parallel", "parallel"),
        name="kv_prep",
    )(*[a for a, _ in kv_srcs], k_gain, cmp_pos, w1, b1, w2)


def _nsa_kernel(slopes_ref, q_ref, bg_ref, gl_ref, kc_ref, vct_ref, ks_ref, vst_ref, kw_ref, vwt_ref,
                qgain_ref, ovt_ref, o_ref, qt_ref, m_ref, acc_ref, *, n_cmp):
    tq = q_ref.shape[0]
    n_grp = kc_ref.shape[0]
    n_slc = ovt_ref.shape[0]
    h = pl.program_id(1)
    i = pl.program_id(2)
    t0 = i * tq

    for g in range(B_GROUP):
        q = q_ref[:, g * HEAD_DIM:(g + 1) * HEAD_DIM]
        qt_ref[g, :HEAD_DIM, :] = (_rms(q, qgain_ref[...]) * (HEAD_DIM ** -0.5)).T.astype(BF16)

    n_idx = lax.broadcasted_iota(jnp.int32, (n_grp, tq), 0)
    t_idx = t0 + lax.broadcasted_iota(jnp.int32, (n_grp, tq), 1)
    dist_c = t_idx - (n_idx * CMP_STRIDE + (CMP_LEN - 1))
    valid_c = (n_idx < n_cmp) & (dist_c >= 0)
    dist_cf = dist_c.astype(F32)
    groups = range(B_GROUP)
    s_c = [_dot(kc_ref[...], qt_ref[g, :HEAD_DIM, :]) for g in groups]
    s_c = [jnp.where(valid_c, s - slopes_ref[h * B_GROUP + g] * dist_cf, NEG_INF) for g, s in zip(groups, s_c)]
    m_c = [jnp.max(s, axis=0, keepdims=True) for s in s_c]
    e_c = [jnp.where(valid_c, jnp.exp(s - m), 0.0) for s, m in zip(s_c, m_c)]
    p_c = [e / jnp.maximum(jnp.sum(e, axis=0, keepdims=True), 1e-30) for e in e_c]
    o_cmp = [_dot(vct_ref[...], p.astype(BF16)) for p in p_c]
    p_sum = sum(p_c[1:], p_c[0])

    imp = jnp.dot(ovt_ref[...], p_sum, precision=lax.Precision.HIGHEST, preferred_element_type=F32)
    j_idx = lax.broadcasted_iota(jnp.int32, (n_slc, tq), 0)
    j_f = j_idx.astype(F32)
    t_sel = t0 + lax.broadcasted_iota(jnp.int32, (n_slc, tq), 1)
    cur = t_sel // SLC_BLOCK
    forced = (j_idx == 0) | (j_idx == cur) | (j_idx == cur - 1)
    val = jnp.where(forced, FORCE_SCORE, jnp.where(j_idx * SLC_BLOCK <= t_sel, imp, -1.0))
    sel_t = jnp.zeros((n_slc, tq), F32)
    for _ in range(min(TOP_N, n_slc)):
        best = jnp.max(val, axis=0, keepdims=True)
        first = jnp.min(jnp.where(val == best, j_f, float(n_slc)), axis=0, keepdims=True)
        hit = j_f == first
        sel_t = jnp.where(hit, 1.0, sel_t)
        val = jnp.where(hit, -jnp.inf, val)

    unselected = (sel_t - 1.0) * (-NEG_INF)
    e_row = lax.broadcasted_iota(jnp.int32, (EXTRA_BLOCK0, tq), 0)
    pad = jnp.zeros((HEAD_DIM - EXTRA_BLOCK0 - n_slc, tq), F32)
    for g in range(B_GROUP):
        slope = slopes_ref[h * B_GROUP + g]
        alibi = jnp.where(e_row == 0, slope * ALIBI_SPLIT, jnp.where(e_row == 1, slope, 0.0))
        qt_ref[g, HEAD_DIM:, :] = jnp.concatenate([alibi, unselected, pad], axis=0).astype(BF16)

    key_r = lax.broadcasted_iota(jnp.int32, (tq, tq), 0)
    qry_c = lax.broadcasted_iota(jnp.int32, (tq, tq), 1)
    causal = key_r <= qry_c

    SLC, WIN = 0, 1
    branch_refs = {SLC: (ks_ref, vst_ref), WIN: (kw_ref, vwt_ref)}

    def tiles_update(tiles, first):
        heads = range(B_GROUP)
        s = []
        for br, kt, mask in tiles:
            k_blk = branch_refs[br][0][pl.ds(pl.multiple_of(kt * tq, tq), tq), :]
            s.append([_dot(k_blk, qt_ref[g]) for g in heads])
        s = [[sg if mask is None else jnp.where(mask, sg, NEG_INF) for sg in st]
             for st, (_, _, mask) in zip(s, tiles)]
        branches = sorted({br for br, _, _ in tiles})
        m_old, m_new = {}, {}
        for br in branches:
            for g in heads:
                m = None
                for st, (tb, _, _) in zip(s, tiles):
                    if tb == br:
                        tile_max = jnp.max(st[g], axis=0, keepdims=True)
                        m = tile_max if m is None else jnp.maximum(m, tile_max)
                if not first:
                    m_old[br, g] = m_ref[br * B_GROUP + g]
                    m = jnp.maximum(m_old[br, g], m)
                m_new[br, g] = m
        p = [[jnp.exp(st[g] - m_new[br, g]).astype(BF16) for g in heads] for st, (br, _, _) in zip(s, tiles)]
        pv = [[_dot(branch_refs[br][1][kt], pt[g]) for g in heads] for pt, (br, kt, _) in zip(p, tiles)]
        for br in branches:
            for g in heads:
                total = None
                for pvt, (tb, _, _) in zip(pv, tiles):
                    if tb == br:
                        total = pvt[g] if total is None else total + pvt[g]
                slot = br * B_GROUP + g
                if first:
                    acc_ref[slot] = total
                else:
                    acc_ref[slot] = jnp.exp(m_old[br, g] - m_new[br, g]) * acc_ref[slot] + total
                m_ref[slot] = m_new[br, g]

    def finish(br):
        return [acc_ref[br * B_GROUP + g, :HEAD_DIM, :] / acc_ref[br * B_GROUP + g, HEAD_DIM:HEAD_DIM + 1, :]
                for g in range(B_GROUP)]

    tiles_update([(SLC, i, causal), (WIN, i, causal)], True)

    def slc_pair(j, carry):
        tiles_update([(SLC, 2 * j, None), (SLC, 2 * j + 1, None)], False)
        return carry

    lax.fori_loop(0, i // 2, slc_pair, 0)

    @pl.when(i % 2 == 1)
    def _():
        tiles_update([(SLC, i - 1, None)], False)

    win_tiles = (WINDOW + tq - 1) // tq

    def win_tile(off):
        needs_mask = off * tq + tq - 1 >= WINDOW
        return (WIN, i - off, ((qry_c - key_r) + off * tq < WINDOW) if needs_mask else None)

    for count in range(1, win_tiles + 1):
        @pl.when(jnp.minimum(i, win_tiles) == count)
        def _(count=count):
            tiles_update([win_tile(off) for off in range(1, count + 1)], False)

    o_slc = finish(SLC)
    o_win = finish(WIN)

    gates_t = jax.nn.sigmoid(gl_ref[...].T)
    for g in range(B_GROUP):
        c = slice(g * HEAD_DIM, (g + 1) * HEAD_DIM)
        o_t = (gates_t[g:g + 1] * o_cmp[g]
               + gates_t[B_GROUP + g:B_GROUP + g + 1] * o_slc[g]
               + gates_t[2 * B_GROUP + g:2 * B_GROUP + g + 1] * o_win[g])
        o_ref[:, c] = (o_t.T * jax.nn.silu(bg_ref[:, c])).astype(BF16)


def _overlap_t(n_grp, n_cmp, n_slc):
    cs = np.arange(n_grp)[None, :] * CMP_STRIDE
    ss = np.arange(n_slc)[:, None] * SLC_BLOCK
    ov = np.clip(np.minimum(cs + CMP_LEN, ss + SLC_BLOCK) - np.maximum(cs, ss), 0, None) / CMP_LEN
    ov = ov * (np.arange(n_grp)[None, :] < n_cmp)
    return jnp.asarray(ov, dtype=F32)


def _nsa_attn(z, zg, kc, vct, ks, vst, kw, vwt, q_gain, batch, seq, q_col0, bg_col0, zg_col0):
    t = z.shape[0]
    tq = min(ATT_TQ, seq)
    assert seq % tq == 0 and tq % SLC_BLOCK == 0 and seq // SLC_BLOCK <= LANES
    nq = seq // tq
    n_grp = seq // CMP_STRIDE
    n_cmp = (seq - CMP_LEN) // CMP_STRIDE + 1
    n_slc = seq // SLC_BLOCK
    gw = B_GROUP * HEAD_DIM
    h_idx = jnp.arange(1, B_HEADS + 1, dtype=F32)
    slopes = jnp.exp2(-8.0 * h_idx / B_HEADS)
    ovt = _overlap_t(n_grp, n_cmp, n_slc)
    rows = lambda b, h, i: b * nq + i
    spec4 = lambda r, c: pl.BlockSpec((None, None, r, c), lambda b, h, i: (b, h, 0, 0))
    vt_spec = pl.BlockSpec((None, None, nq, VT_ROWS, tq), lambda b, h, i: (b, h, 0, 0, 0))
    full = lambda a: pl.BlockSpec(a.shape, lambda b, h, i: (0,) * a.ndim)
    qg = q_gain.reshape(1, HEAD_DIM)
    return pl.pallas_call(
        functools.partial(_nsa_kernel, n_cmp=n_cmp),
        grid=(batch, B_KV_HEADS, nq),
        in_specs=[
            pl.BlockSpec(memory_space=pltpu.SMEM),
            pl.BlockSpec((tq, gw), lambda b, h, i: (rows(b, h, i), q_col0 + h)),
            pl.BlockSpec((tq, gw), lambda b, h, i: (rows(b, h, i), bg_col0 + h)),
            pl.BlockSpec((tq, LANES), lambda b, h, i: (rows(b, h, i), zg_col0 + h)),
            spec4(n_grp, HEAD_DIM), spec4(HEAD_DIM, n_grp), spec4(seq, KEY_COLS), vt_spec,
            spec4(seq, KEY_COLS), vt_spec, full(qg), full(ovt),
        ],
        out_specs=pl.BlockSpec((tq, gw), lambda b, h, i: (rows(b, h, i), h)),
        out_shape=jax.ShapeDtypeStruct((t, B_WIDTH), BF16),
        scratch_shapes=[
            pltpu.VMEM((B_GROUP, KEY_COLS, tq), BF16),
            pltpu.VMEM((2 * B_GROUP, 1, tq), F32),
            pltpu.VMEM((2 * B_GROUP, VT_ROWS, tq), F32),
        ],
        compiler_params=_params("parallel", "parallel", "arbitrary"),
        name="nsa_attn",
    )(slopes, z, z, zg, kc, vct, ks, vst, kw, vwt, qg, ovt)


def _hgrn_kernel(q_ref, f_ref, i_ref, gate_ref, lb_ref, og_ref, o_ref, state_ref, *, layer):
    n_rows = q_ref.shape[0]
    n_heads = q_ref.shape[1] // C_DIM
    mid = C_CHUNK // 2 - 1

    logits = lb_ref[...]
    e = jnp.exp(logits - jnp.max(logits, axis=0, keepdims=True))
    p = e / jnp.sum(e, axis=0, keepdims=True)
    lb = jnp.sum(p[:layer + 1], axis=0, keepdims=True) - p[0:1]

    r_i = lax.broadcasted_iota(jnp.int32, (C_CHUNK, C_CHUNK), 0)
    c_i = lax.broadcasted_iota(jnp.int32, (C_CHUNK, C_CHUNK), 1)
    lower = r_i >= c_i
    tri = jnp.where(lower, 1.0, 0.0).astype(BF16)

    @pl.when(pl.program_id(2) == 0)
    def _():
        state_ref[...] = jnp.zeros_like(state_ref)

    def body(c, carry):
        r0 = pl.multiple_of(c * C_CHUNK, C_CHUNK)
        rows = pl.ds(r0, C_CHUNK)
        f = lb + (1.0 - lb) * jax.nn.sigmoid(f_ref[rows, :])
        k = 1.0 - f
        g = jnp.log(f)
        g_hi = g.astype(BF16)
        g_rest = g - g_hi.astype(F32)
        g_mid = g_rest.astype(BF16)
        g_lo = (g_rest - g_mid.astype(F32)).astype(BF16)
        gcum = _dot(tri, g_hi) + _dot(tri, g_mid) + _dot(tri, g_lo)
        g_mid = gcum[mid:mid + 1, :]
        g_end = gcum[C_CHUNK - 1:C_CHUNK, :]
        qa = q_ref[rows, :] * jnp.exp(gcum - g_mid)
        kb = k * jnp.exp(g_mid - gcum)
        q_in = (qa * jnp.exp(g_mid)).astype(BF16)
        k_st = (kb * jnp.exp(g_end - g_mid)).astype(BF16)
        qa = qa.astype(BF16)
        kb = kb.astype(BF16)
        v = i_ref[rows, :].astype(BF16)
        decay = jnp.exp(g_end)
        heads = range(n_heads)
        cs = [slice(hh * C_DIM, (hh + 1) * C_DIM) for hh in heads]
        attn = [_dot_nt(qa[:, c], kb[:, c]) for c in cs]
        state = [state_ref[hh] for hh in heads]
        inter = [_dot(q_in[:, c], s.astype(BF16)) for c, s in zip(cs, state)]
        update = [_dot_tn(k_st[:, c], v[:, c]) for c in cs]
        attn = [jnp.where(lower, a, 0.0).astype(BF16) for a in attn]
        o = [_dot(a, v[:, c]) + x for a, c, x in zip(attn, cs, inter)]
        for hh in heads:
            decay_rows = jnp.broadcast_to(decay[:, cs[hh]], (C_DIM, C_DIM)).T
            state_ref[hh] = decay_rows * state[hh] + update[hh]
        for hh in heads:
            out = _rms(o[hh], og_ref[...]) * jax.nn.silu(gate_ref[rows, cs[hh]])
            o_ref[rows, cs[hh]] = out.astype(BF16)
        return carry

    lax.fori_loop(0, n_rows // C_CHUNK, body, 0, unroll=2)


def _hgrn(z, lb_logits, out_gain, batch, seq, layer):
    t = z.shape[0]
    width = C_HEADS * C_DIM
    bw = HGRN_HEADS * C_DIM
    nb = width // bw
    rows = min(HGRN_ROWS, seq)
    ns = seq // rows
    assert seq % rows == 0 and rows % C_CHUNK == 0
    depth = lb_logits.shape[0]
    col = lambda part: pl.BlockSpec((rows, bw), lambda b, j, s, part=part: (b * ns + s, part * nb + j))
    og = out_gain.reshape(1, C_DIM)
    return pl.pallas_call(
        functools.partial(_hgrn_kernel, layer=layer),
        grid=(batch, nb, ns),
        in_specs=[col(0), col(1), col(2), col(3),
                  pl.BlockSpec((depth, bw), lambda b, j, s: (0, j)),
                  pl.BlockSpec((1, C_DIM), lambda b, j, s: (0, 0))],
        out_specs=pl.BlockSpec((rows, bw), lambda b, j, s: (b * ns + s, j)),
        out_shape=jax.ShapeDtypeStruct((t, width), BF16),
        scratch_shapes=[pltpu.VMEM((HGRN_HEADS, C_DIM, C_DIM), F32)],
        compiler_params=_params("parallel", "parallel", "arbitrary"),
        name="hgrn",
    )(z, z, z, z, lb_logits, og)


def _even_layer(h, batch, seq, norm_g, w_in, ln_g, ln_b, sgu_w, sgu_b, q_gain, k_gain,
                cmp_pos, cmp_w1, cmp_b1, cmp_w2, w_out):
    d = h.shape[1]
    n_main = 3 * A_WIDTH + 2 * B_WIDTH + 6 * B_KV_HEADS * HEAD_DIM
    wg = w_in[:, n_main:].reshape(d, 3, B_KV_HEADS, B_GROUP).transpose(0, 2, 1, 3)
    wg = wg.reshape(d, B_KV_HEADS, 3 * B_GROUP)
    wg = jnp.pad(wg, ((0, 0), (0, 0), (0, LANES - 3 * B_GROUP))).reshape(d, B_KV_HEADS * LANES)
    n_tiled = n_main // PROJ_TN * PROJ_TN
    w_tail = jnp.concatenate([w_in[:, n_tiled:n_main], wg], axis=1)
    z, z_tail = _rms_matmul(h, norm_g, w_in[:, :n_tiled].astype(BF16), w_tail.astype(BF16))

    sgu_b_full = jnp.broadcast_to(sgu_b[:, :, None], (A_GROUPS, A_CHUNK, HEAD_DIM))
    a_out = _mixer_a(z, ln_g, ln_b, sgu_w, sgu_b_full)

    gw = B_GROUP * HEAD_DIM
    q_col0 = 3 * A_WIDTH // gw
    bg_col0 = (3 * A_WIDTH + B_WIDTH) // gw
    assert (3 * A_WIDTH + 2 * B_WIDTH) <= n_tiled and n_tiled % (B_KV_HEADS * HEAD_DIM) == 0
    kv_srcs = []
    for i in range(6):
        col = 3 * A_WIDTH + 2 * B_WIDTH + i * B_KV_HEADS * HEAD_DIM
        kv_srcs.append((z, col // HEAD_DIM) if col < n_tiled else (z_tail, (col - n_tiled) // HEAD_DIM))
    zg_col0 = (n_main - n_tiled) // LANES
    w1 = cmp_w1.reshape(2, CMP_LEN, HEAD_DIM, CMP_HIDDEN).astype(BF16)
    kc, vct, ks, vst, kw, vwt = _kv_prep(kv_srcs, batch, seq, k_gain, cmp_pos, w1,
                                         cmp_b1.reshape(2, 1, CMP_HIDDEN), cmp_w2.astype(BF16))
    b_out = _nsa_attn(z, z_tail, kc, vct, ks, vst, kw, vwt, q_gain, batch, seq, q_col0, bg_col0, zg_col0)

    w_out = w_out.astype(BF16)
    return _matmul_residual(h, [a_out, b_out], [w_out[:A_WIDTH], w_out[A_WIDTH:]])


def _odd_layer(h, batch, seq, layer, norm_g, w_in, lb_logits, out_gain, w_out):
    z = _rms_matmul(h, norm_g, w_in.astype(BF16))
    o = _hgrn(z, lb_logits, out_gain, batch, seq, layer)
    return _matmul_residual(h, [o], [w_out.astype(BF16)])


def kernel(x, even_norm, even_w_in, sgu_ln_g, sgu_ln_b, sgu_w, sgu_b, nsa_q_gain, nsa_k_gain, cmp_pos, cmp_w1, cmp_b1, cmp_w2, even_w_out, odd_norm, odd_w_in, hgrn_lb, hgrn_out_gain, odd_w_out):
    batch, seq, d = x.shape
    depth = hgrn_lb.shape[0]
    h = x.reshape(batch * seq, d)
    for layer in range(depth):
        if layer % 2 == 0:
            e = layer // 2
            h = _even_layer(h, batch, seq, even_norm[e], even_w_in[e], sgu_ln_g[e], sgu_ln_b[e],
                            sgu_w[e], sgu_b[e], nsa_q_gain[e], nsa_k_gain[e], cmp_pos[e], cmp_w1[e],
                            cmp_b1[e], cmp_w2[e], even_w_out[e])
        else:
            o = layer // 2
            h = _odd_layer(h, batch, seq, layer, odd_norm[o], odd_w_in[o], hgrn_lb, hgrn_out_gain[o],
                           odd_w_out[o])
    return h.reshape(batch, seq, d)
```

```python
import functools

import numpy as np
import jax
import jax.numpy as jnp
from jax import lax
from jax.experimental import pallas as pl
from jax.experimental.pallas import tpu as pltpu

HEAD_DIM = 128
A_GROUPS = 8
A_WIDTH = A_GROUPS * HEAD_DIM
A_CHUNK = 128
B_HEADS = 8
B_KV_HEADS = 2
B_GROUP = B_HEADS // B_KV_HEADS
B_WIDTH = B_HEADS * HEAD_DIM
CMP_LEN = 32
CMP_STRIDE = 16
CMP_HIDDEN = 256
SLC_BLOCK = 64
TOP_N = 8
WINDOW = 512
C_HEADS = 16
C_DIM = 128
C_CHUNK = 64
EPS = 1e-6
NEG_INF = -1e30
FORCE_SCORE = 1e6

LANES = 128
BF16_SUBLANES = 16
VMEM_LIMIT_BYTES = 56 * 1024 * 1024

PROJ_TM = 1024
PROJ_TN = 1024
PLAIN_TN = 2048
RESID_TM = 512
RESID_TN = 2048
NORM_ROWS = 256
MIXA_ROWS = 512
ATT_TQ = 256
VT_ROWS = HEAD_DIM + BF16_SUBLANES
KEY_COLS = 2 * HEAD_DIM
ALIBI_SPLIT = 64
EXTRA_BLOCK0 = BF16_SUBLANES
HGRN_HEADS = 8
HGRN_ROWS = 512

BF16 = jnp.bfloat16
F32 = jnp.float32


def _params(*sem):
    return pltpu.CompilerParams(dimension_semantics=sem, vmem_limit_bytes=VMEM_LIMIT_BYTES)


def _col_tile(n, limit):
    if n <= limit:
        return n
    return max(c for c in range(LANES, limit + 1, LANES) if n % c == 0)


def _dot(a, b):
    return jnp.dot(a, b, preferred_element_type=F32)


def _dot_nt(a, b):
    return lax.dot_general(a, b, (((1,), (1,)), ((), ())), preferred_element_type=F32)


def _dot_tn(a, b):
    return lax.dot_general(a, b, (((0,), (0,)), ((), ())), preferred_element_type=F32)


def _rms(x, gain):
    return x * lax.rsqrt(jnp.mean(x * x, axis=-1, keepdims=True) + EPS) * gain


def _rms_matmul_kernel(x_ref, g_ref, w_ref, *rest, has_extra):
    if has_extra:
        w2_ref, o_ref, o2_ref, xn_ref = rest
    else:
        o_ref, xn_ref = rest

    @pl.when(pl.program_id(1) == 0)
    def _():
        def body(r, carry):
            r0 = pl.multiple_of(r * NORM_ROWS, NORM_ROWS)
            x = x_ref[pl.ds(r0, NORM_ROWS), :]
            xn_ref[pl.ds(r0, NORM_ROWS), :] = _rms(x, g_ref[...]).astype(BF16)
            return carry

        lax.fori_loop(0, x_ref.shape[0] // NORM_ROWS, body, 0)
        if has_extra:
            o2_ref[...] = _dot(xn_ref[...], w2_ref[...])

    o_ref[...] = _dot(xn_ref[...], w_ref[...])


def _rms_matmul(x, gain, w, w_extra=None):
    t, d = x.shape
    n = w.shape[1]
    tm = min(PROJ_TM, t)
    tn = _col_tile(n, PROJ_TN)
    assert t % tm == 0 and n % tn == 0 and tm % NORM_ROWS == 0
    in_specs = [
        pl.BlockSpec((tm, d), lambda i, j: (i, 0)),
        pl.BlockSpec((1, d), lambda i, j: (0, 0)),
        pl.BlockSpec((d, tn), lambda i, j: (0, j)),
    ]
    out_shape = [jax.ShapeDtypeStruct((t, n), F32)]
    out_specs = [pl.BlockSpec((tm, tn), lambda i, j: (i, j))]
    args = [x, gain.reshape(1, d), w]
    if w_extra is not None:
        n2 = w_extra.shape[1]
        in_specs.append(pl.BlockSpec((d, n2), lambda i, j: (0, 0)))
        out_shape.append(jax.ShapeDtypeStruct((t, n2), F32))
        out_specs.append(pl.BlockSpec((tm, n2), lambda i, j: (i, 0)))
        args.append(w_extra)
    out = pl.pallas_call(
        functools.partial(_rms_matmul_kernel, has_extra=w_extra is not None),
        grid=(t // tm, n // tn),
        in_specs=in_specs,
        out_specs=out_specs,
        out_shape=out_shape,
        scratch_shapes=[pltpu.VMEM((tm, d), BF16)],
        compiler_params=_params("parallel", "arbitrary"),
        name="rms_matmul",
    )(*args)
    return out if w_extra is not None else out[0]


def _matmul_residual_kernel(x_ref, *refs, n_pairs, emit_norm):
    acts, ws, rest = refs[:n_pairs], refs[n_pairs:2 * n_pairs], refs[2 * n_pairs:]
    acc = x_ref[...]
    for a_ref, w_ref in zip(acts, ws):
        acc = acc + _dot(a_ref[...], w_ref[...])
    if emit_norm:
        gain_ref, o_ref, xn_ref = rest
        xn_ref[...] = _rms(acc, gain_ref[...]).astype(BF16)
    else:
        (o_ref,) = rest
    o_ref[...] = acc


def _matmul_residual(x, acts, ws, next_gain=None):
    t, d = x.shape
    tm = min(RESID_TM, t)
    tn = _col_tile(d, RESID_TN)
    assert t % tm == 0 and d % tn == 0
    emit_norm = next_gain is not None and tn == d
    in_specs = [pl.BlockSpec((tm, tn), lambda i, j: (i, j))]
    in_specs += [pl.BlockSpec((tm, a.shape[1]), lambda i, j: (i, 0)) for a in acts]
    in_specs += [pl.BlockSpec((w.shape[0], tn), lambda i, j: (0, j)) for w in ws]
    out_specs = [pl.BlockSpec((tm, tn), lambda i, j: (i, j))]
    out_shape = [jax.ShapeDtypeStruct((t, d), F32)]
    args = [x, *acts, *ws]
    if emit_norm:
        in_specs.append(pl.BlockSpec((1, d), lambda i, j: (0, 0)))
        out_specs.append(pl.BlockSpec((tm, d), lambda i, j: (i, 0)))
        out_shape.append(jax.ShapeDtypeStruct((t, d), BF16))
        args.append(next_gain.reshape(1, d))
    out = pl.pallas_call(
        functools.partial(_matmul_residual_kernel, n_pairs=len(acts), emit_norm=emit_norm),
        grid=(t // tm, d // tn),
        in_specs=in_specs,
        out_specs=out_specs,
        out_shape=out_shape,
        compiler_params=_params("parallel", "arbitrary"),
        name="matmul_residual",
    )(*args)
    return (out[0], out[1]) if emit_norm else (out[0], None)


def _matmul_kernel(x_ref, w_ref, o_ref):
    o_ref[...] = _dot(x_ref[...], w_ref[...])


def _matmul(xn, w):
    t, d = xn.shape
    n = w.shape[1]
    tm = min(PROJ_TM, t)
    tn = _col_tile(n, PLAIN_TN)
    assert t % tm == 0 and n % tn == 0
    return pl.pallas_call(
        _matmul_kernel,
        grid=(t // tm, n // tn),
        in_specs=[pl.BlockSpec((tm, d), lambda i, j: (i, 0)), pl.BlockSpec((d, tn), lambda i, j: (0, j))],
        out_specs=pl.BlockSpec((tm, tn), lambda i, j: (i, j)),
        out_shape=jax.ShapeDtypeStruct((t, n), F32),
        compiler_params=_params("parallel", "arbitrary"),
        name="matmul",
    )(xn, w)


def _mixer_a_kernel(u_ref, v_ref, gate_ref, lng_ref, lnb_ref, w_ref, b_ref, o_ref):
    rows = u_ref.shape[0]
    v = jax.nn.gelu(v_ref[...])
    mu = jnp.mean(v, axis=-1, keepdims=True)
    vc = v - mu
    var = jnp.mean(vc * vc, axis=-1, keepdims=True)
    vn = (vc * lax.rsqrt(var + EPS) * lng_ref[...] + lnb_ref[...]).astype(BF16)
    tr = lax.broadcasted_iota(jnp.int32, (A_CHUNK, A_CHUNK), 0)
    tc = lax.broadcasted_iota(jnp.int32, (A_CHUNK, A_CHUNK), 1)
    for g in range(A_GROUPS):
        cols = slice(g * HEAD_DIM, (g + 1) * HEAD_DIM)
        wm = jnp.where(tr >= tc, w_ref[g], 0.0).astype(BF16)
        for c in range(rows // A_CHUNK):
            rs = slice(c * A_CHUNK, (c + 1) * A_CHUNK)
            sv = _dot(wm, vn[rs, cols]) + b_ref[g]
            u = jax.nn.gelu(u_ref[rs, cols])
            o_ref[rs, cols] = (u * sv * jax.nn.silu(gate_ref[rs, cols])).astype(BF16)


def _mixer_a(z, ln_g, ln_b, sgu_w, sgu_b_full):
    t = z.shape[0]
    rows = min(MIXA_ROWS, t)
    assert t % rows == 0 and rows % A_CHUNK == 0
    blk = lambda c: pl.BlockSpec((rows, A_WIDTH), lambda i: (i, c))
    full = lambda a: pl.BlockSpec(a.shape, lambda i: (0,) * a.ndim)
    lng = ln_g.reshape(1, A_WIDTH)
    lnb = ln_b.reshape(1, A_WIDTH)
    return pl.pallas_call(
        _mixer_a_kernel,
        grid=(t // rows,),
        in_specs=[blk(0), blk(1), blk(2), full(lng), full(lnb), full(sgu_w), full(sgu_b_full)],
        out_specs=pl.BlockSpec((rows, A_WIDTH), lambda i: (i, 0)),
        out_shape=jax.ShapeDtypeStruct((t, A_WIDTH), BF16),
        compiler_params=_params("parallel"),
        name="mixer_a",
    )(z, z, z, lng, lnb, sgu_w, sgu_b_full)


def _compress(src_ref, pos_ref, w1_ref, b1_ref, w2_ref):
    groups = src_ref.shape[0] // CMP_STRIDE
    lo = jnp.zeros((groups, CMP_HIDDEN), F32)
    hi = jnp.zeros((groups, CMP_HIDDEN), F32)
    for l in range(CMP_STRIDE):
        xl = src_ref[pl.ds(l, groups, stride=CMP_STRIDE), :]
        lo = lo + _dot((xl + pos_ref[l:l + 1, :]).astype(BF16), w1_ref[l])
        hi = hi + _dot((xl + pos_ref[CMP_STRIDE + l:CMP_STRIDE + l + 1, :]).astype(BF16),
                       w1_ref[CMP_STRIDE + l])
    hid = jax.nn.gelu(lo + pltpu.roll(hi, groups - 1, 0) + b1_ref[...])
    return _dot(hid.astype(BF16), w2_ref[...])


def _store_values_t(v_src, vt_ref):
    n_tiles, rows, tk = vt_ref.shape
    for kt in range(n_tiles):
        vt_ref[kt, :HEAD_DIM, :] = v_src[kt * tk:(kt + 1) * tk, :].T.astype(BF16)
        vt_ref[kt, HEAD_DIM:, :] = jnp.ones((rows - HEAD_DIM, tk), BF16)


def _key_extras(seq, with_blocks):
    kpos = lax.broadcasted_iota(jnp.int32, (seq, LANES), 0)
    c = lax.broadcasted_iota(jnp.int32, (seq, LANES), 1)
    hi = (kpos // ALIBI_SPLIT).astype(F32)
    lo = (kpos % ALIBI_SPLIT).astype(F32)
    ex = jnp.where(c == 0, hi, jnp.where(c == 1, lo, 0.0))
    if with_blocks:
        ex = jnp.where(kpos // SLC_BLOCK == c - EXTRA_BLOCK0, 1.0, ex)
    return ex.astype(BF16)


def _kv_prep_kernel(kc_src, vc_src, ks_src, vs_src, kw_src, vw_src, kgain_ref, pos_ref, w1_ref,
                    b1_ref, w2_ref, kc_ref, vct_ref, ks_ref, vst_ref, kw_ref, vwt_ref):
    seq = ks_src.shape[0]
    ks_ref[:, :HEAD_DIM] = _rms(ks_src[...], kgain_ref[1:2, :]).astype(BF16)
    ks_ref[:, HEAD_DIM:] = _key_extras(seq, True)
    kw_ref[:, :HEAD_DIM] = _rms(kw_src[...], kgain_ref[2:3, :]).astype(BF16)
    kw_ref[:, HEAD_DIM:] = _key_extras(seq, False)
    _store_values_t(vs_src, vst_ref)
    _store_values_t(vw_src, vwt_ref)
    kc = _compress(kc_src, pos_ref.at[0], w1_ref.at[0], b1_ref.at[0], w2_ref.at[0])
    kc_ref[...] = _rms(kc, kgain_ref[0:1, :]).astype(BF16)
    vc = _compress(vc_src, pos_ref.at[1], w1_ref.at[1], b1_ref.at[1], w2_ref.at[1])
    vct_ref[...] = vc.T.astype(BF16)


def _kv_prep(kv_srcs, batch, seq, k_gain, cmp_pos, w1, b1, w2):
    groups = seq // CMP_STRIDE
    tk = min(ATT_TQ, seq)
    src = lambda i: pl.BlockSpec((seq, HEAD_DIM), lambda b, h, c0=kv_srcs[i][1]: (b, c0 + h))
    full = lambda a: pl.BlockSpec(a.shape, lambda b, h: (0,) * a.ndim)
    spec4 = lambda r, c: pl.BlockSpec((None, None, r, c), lambda b, h: (b, h, 0, 0))
    shape4 = lambda r, c: jax.ShapeDtypeStruct((batch, B_KV_HEADS, r, c), BF16)
    vt_spec = pl.BlockSpec((None, None, seq // tk, VT_ROWS, tk), lambda b, h: (b, h, 0, 0, 0))
    vt_shape = jax.ShapeDtypeStruct((batch, B_KV_HEADS, seq // tk, VT_ROWS, tk), BF16)
    return pl.pallas_call(
        _kv_prep_kernel,
        grid=(batch, B_KV_HEADS),
        in_specs=[src(i) for i in range(6)] + [full(k_gain), full(cmp_pos), full(w1), full(b1), full(w2)],
        out_specs=[spec4(groups, HEAD_DIM), spec4(HEAD_DIM, groups), spec4(seq, KEY_COLS), vt_spec,
                   spec4(seq, KEY_COLS), vt_spec],
        out_shape=[shape4(groups, HEAD_DIM), shape4(HEAD_DIM, groups), shape4(seq, KEY_COLS), vt_shape,
                   shape4(seq, KEY_COLS), vt_shape],
        compiler_params=_params("parallel", "parallel"),
        name="kv_prep",
    )(*[a for a, _ in kv_srcs], k_gain, cmp_pos, w1, b1, w2)


def _nsa_kernel(slopes_ref, q_ref, bg_ref, gl_ref, kc_ref, vct_ref, ks_ref, vst_ref, kw_ref, vwt_ref,
                qgain_ref, ovt_ref, o_ref, qt_ref, m_ref, acc_ref, *, n_cmp):
    tq = q_ref.shape[0]
    n_grp = kc_ref.shape[0]
    n_slc = ovt_ref.shape[0]
    h = pl.program_id(1)
    i = pl.program_id(2)
    t0 = i * tq

    for g in range(B_GROUP):
        q = q_ref[:, g * HEAD_DIM:(g + 1) * HEAD_DIM]
        qt_ref[g, :HEAD_DIM, :] = (_rms(q, qgain_ref[...]) * (HEAD_DIM ** -0.5)).T.astype(BF16)

    n_idx = lax.broadcasted_iota(jnp.int32, (n_grp, tq), 0)
    t_idx = t0 + lax.broadcasted_iota(jnp.int32, (n_grp, tq), 1)
    dist_c = t_idx - (n_idx * CMP_STRIDE + (CMP_LEN - 1))
    valid_c = (n_idx < n_cmp) & (dist_c >= 0)
    dist_cf = dist_c.astype(F32)
    groups = range(B_GROUP)
    s_c = [_dot(kc_ref[...], qt_ref[g, :HEAD_DIM, :]) for g in groups]
    s_c = [jnp.where(valid_c, s - slopes_ref[h * B_GROUP + g] * dist_cf, NEG_INF) for g, s in zip(groups, s_c)]
    m_c = [jnp.max(s, axis=0, keepdims=True) for s in s_c]
    e_c = [jnp.where(valid_c, jnp.exp(s - m), 0.0) for s, m in zip(s_c, m_c)]
    p_c = [e / jnp.maximum(jnp.sum(e, axis=0, keepdims=True), 1e-30) for e in e_c]
    o_cmp = [_dot(vct_ref[...], p.astype(BF16)) for p in p_c]
    p_sum = sum(p_c[1:], p_c[0])

    imp = jnp.dot(ovt_ref[...], p_sum, precision=lax.Precision.HIGHEST, preferred_element_type=F32)
    j_idx = lax.broadcasted_iota(jnp.int32, (n_slc, tq), 0)
    j_f = j_idx.astype(F32)
    t_sel = t0 + lax.broadcasted_iota(jnp.int32, (n_slc, tq), 1)
    cur = t_sel // SLC_BLOCK
    forced = (j_idx == 0) | (j_idx == cur) | (j_idx == cur - 1)
    val = jnp.where(forced, FORCE_SCORE, jnp.where(j_idx * SLC_BLOCK <= t_sel, imp, -1.0))
    sel_t = jnp.zeros((n_slc, tq), F32)
    for _ in range(min(TOP_N, n_slc)):
        best = jnp.max(val, axis=0, keepdims=True)
        first = jnp.min(jnp.where(val == best, j_f, float(n_slc)), axis=0, keepdims=True)
        hit = j_f == first
        sel_t = jnp.where(hit, 1.0, sel_t)
        val = jnp.where(hit, -jnp.inf, val)

    unselected = (sel_t - 1.0) * (-NEG_INF)
    e_row = lax.broadcasted_iota(jnp.int32, (EXTRA_BLOCK0, tq), 0)
    pad = jnp.zeros((HEAD_DIM - EXTRA_BLOCK0 - n_slc, tq), F32)
    for g in range(B_GROUP):
        slope = slopes_ref[h * B_GROUP + g]
        alibi = jnp.where(e_row == 0, slope * ALIBI_SPLIT, jnp.where(e_row == 1, slope, 0.0))
        qt_ref[g, HEAD_DIM:, :] = jnp.concatenate([alibi, unselected, pad], axis=0).astype(BF16)

    key_r = lax.broadcasted_iota(jnp.int32, (tq, tq), 0)
    qry_c = lax.broadcasted_iota(jnp.int32, (tq, tq), 1)
    causal = key_r <= qry_c

    SLC, WIN = 0, 1
    branch_refs = {SLC: (ks_ref, vst_ref), WIN: (kw_ref, vwt_ref)}

    def tiles_update(tiles, first):
        heads = range(B_GROUP)
        s = []
        for br, kt, mask in tiles:
            k_blk = branch_refs[br][0][pl.ds(pl.multiple_of(kt * tq, tq), tq), :]
            s.append([_dot(k_blk, qt_ref[g]) for g in heads])
        s = [[sg if mask is None else jnp.where(mask, sg, NEG_INF) for sg in st]
             for st, (_, _, mask) in zip(s, tiles)]
        branches = sorted({br for br, _, _ in tiles})
        m_old, m_new = {}, {}
        for br in branches:
            for g in heads:
                m = None
                for st, (tb, _, _) in zip(s, tiles):
                    if tb == br:
                        tile_max = jnp.max(st[g], axis=0, keepdims=True)
                        m = tile_max if m is None else jnp.maximum(m, tile_max)
                if not first:
                    m_old[br, g] = m_ref[br * B_GROUP + g]
                    m = jnp.maximum(m_old[br, g], m)
                m_new[br, g] = m
        p = [[jnp.exp(st[g] - m_new[br, g]).astype(BF16) for g in heads] for st, (br, _, _) in zip(s, tiles)]
        pv = [[_dot(branch_refs[br][1][kt], pt[g]) for g in heads] for pt, (br, kt, _) in zip(p, tiles)]
        for br in branches:
            for g in heads:
                total = None
                for pvt, (tb, _, _) in zip(pv, tiles):
                    if tb == br:
                        total = pvt[g] if total is None else total + pvt[g]
                slot = br * B_GROUP + g
                if first:
                    acc_ref[slot] = total
                else:
                    acc_ref[slot] = jnp.exp(m_old[br, g] - m_new[br, g]) * acc_ref[slot] + total
                m_ref[slot] = m_new[br, g]

    def finish(br):
        return [acc_ref[br * B_GROUP + g, :HEAD_DIM, :] / acc_ref[br * B_GROUP + g, HEAD_DIM:HEAD_DIM + 1, :]
                for g in range(B_GROUP)]

    tiles_update([(SLC, i, causal), (WIN, i, causal)], True)

    def slc_pair(j, carry):
        tiles_update([(SLC, 2 * j, None), (SLC, 2 * j + 1, None)], False)
        return carry

    lax.fori_loop(0, i // 2, slc_pair, 0)

    @pl.when(i % 2 == 1)
    def _():
        tiles_update([(SLC, i - 1, None)], False)

    win_tiles = (WINDOW + tq - 1) // tq

    def win_tile(off):
        needs_mask = off * tq + tq - 1 >= WINDOW
        return (WIN, i - off, ((qry_c - key_r) + off * tq < WINDOW) if needs_mask else None)

    for count in range(1, win_tiles + 1):
        @pl.when(jnp.minimum(i, win_tiles) == count)
        def _(count=count):
            tiles_update([win_tile(off) for off in range(1, count + 1)], False)

    o_slc = finish(SLC)
    o_win = finish(WIN)

    gates_t = jax.nn.sigmoid(gl_ref[...].T)
    for g in range(B_GROUP):
        c = slice(g * HEAD_DIM, (g + 1) * HEAD_DIM)
        o_t = (gates_t[g:g + 1] * o_cmp[g]
               + gates_t[B_GROUP + g:B_GROUP + g + 1] * o_slc[g]
               + gates_t[2 * B_GROUP + g:2 * B_GROUP + g + 1] * o_win[g])
        o_ref[:, c] = (o_t.T * jax.nn.silu(bg_ref[:, c])).astype(BF16)


def _overlap_t(n_grp, n_cmp, n_slc):
    cs = np.arange(n_grp)[None, :] * CMP_STRIDE
    ss = np.arange(n_slc)[:, None] * SLC_BLOCK
    ov = np.clip(np.minimum(cs + CMP_LEN, ss + SLC_BLOCK) - np.maximum(cs, ss), 0, None) / CMP_LEN
    ov = ov * (np.arange(n_grp)[None, :] < n_cmp)
    return jnp.asarray(ov, dtype=F32)


def _nsa_attn(z, zg, kc, vct, ks, vst, kw, vwt, q_gain, batch, seq, q_col0, bg_col0, zg_col0):
    t = z.shape[0]
    tq = min(ATT_TQ, seq)
    assert seq % tq == 0 and tq % SLC_BLOCK == 0 and seq // SLC_BLOCK <= LANES
    nq = seq // tq
    n_grp = seq // CMP_STRIDE
    n_cmp = (seq - CMP_LEN) // CMP_STRIDE + 1
    n_slc = seq // SLC_BLOCK
    gw = B_GROUP * HEAD_DIM
    h_idx = jnp.arange(1, B_HEADS + 1, dtype=F32)
    slopes = jnp.exp2(-8.0 * h_idx / B_HEADS)
    ovt = _overlap_t(n_grp, n_cmp, n_slc)
    rows = lambda b, h, i: b * nq + i
    spec4 = lambda r, c: pl.BlockSpec((None, None, r, c), lambda b, h, i: (b, h, 0, 0))
    vt_spec = pl.BlockSpec((None, None, nq, VT_ROWS, tq), lambda b, h, i: (b, h, 0, 0, 0))
    full = lambda a: pl.BlockSpec(a.shape, lambda b, h, i: (0,) * a.ndim)
    qg = q_gain.reshape(1, HEAD_DIM)
    return pl.pallas_call(
        functools.partial(_nsa_kernel, n_cmp=n_cmp),
        grid=(batch, B_KV_HEADS, nq),
        in_specs=[
            pl.BlockSpec(memory_space=pltpu.SMEM),
            pl.BlockSpec((tq, gw), lambda b, h, i: (rows(b, h, i), q_col0 + h)),
            pl.BlockSpec((tq, gw), lambda b, h, i: (rows(b, h, i), bg_col0 + h)),
            pl.BlockSpec((tq, LANES), lambda b, h, i: (rows(b, h, i), zg_col0 + h)),
            spec4(n_grp, HEAD_DIM), spec4(HEAD_DIM, n_grp), spec4(seq, KEY_COLS), vt_spec,
            spec4(seq, KEY_COLS), vt_spec, full(qg), full(ovt),
        ],
        out_specs=pl.BlockSpec((tq, gw), lambda b, h, i: (rows(b, h, i), h)),
        out_shape=jax.ShapeDtypeStruct((t, B_WIDTH), BF16),
        scratch_shapes=[
            pltpu.VMEM((B_GROUP, KEY_COLS, tq), BF16),
            pltpu.VMEM((2 * B_GROUP, 1, tq), F32),
            pltpu.VMEM((2 * B_GROUP, VT_ROWS, tq), F32),
        ],
        compiler_params=_params("parallel", "parallel", "arbitrary"),
        name="nsa_attn",
    )(slopes, z, z, zg, kc, vct, ks, vst, kw, vwt, qg, ovt)


def _hgrn_kernel(q_ref, f_ref, i_ref, gate_ref, lb_ref, og_ref, o_ref, state_ref, *, layer):
    n_rows = q_ref.shape[0]
    n_heads = q_ref.shape[1] // C_DIM
    mid = C_CHUNK // 2 - 1

    logits = lb_ref[...]
    e = jnp.exp(logits - jnp.max(logits, axis=0, keepdims=True))
    p = e / jnp.sum(e, axis=0, keepdims=True)
    lb = jnp.sum(p[:layer + 1], axis=0, keepdims=True) - p[0:1]

    r_i = lax.broadcasted_iota(jnp.int32, (C_CHUNK, C_CHUNK), 0)
    c_i = lax.broadcasted_iota(jnp.int32, (C_CHUNK, C_CHUNK), 1)
    lower = r_i >= c_i
    tri = jnp.where(lower, 1.0, 0.0).astype(BF16)

    @pl.when(pl.program_id(2) == 0)
    def _():
        state_ref[...] = jnp.zeros_like(state_ref)

    def body(c, carry):
        r0 = pl.multiple_of(c * C_CHUNK, C_CHUNK)
        rows = pl.ds(r0, C_CHUNK)
        f = lb + (1.0 - lb) * jax.nn.sigmoid(f_ref[rows, :])
        k = 1.0 - f
        g = jnp.log(f)
        g_hi = g.astype(BF16)
        g_rest = g - g_hi.astype(F32)
        g_mid = g_rest.astype(BF16)
        g_lo = (g_rest - g_mid.astype(F32)).astype(BF16)
        gcum = _dot(tri, g_hi) + _dot(tri, g_mid) + _dot(tri, g_lo)
        g_mid = gcum[mid:mid + 1, :]
        g_end = gcum[C_CHUNK - 1:C_CHUNK, :]
        qa = q_ref[rows, :] * jnp.exp(gcum - g_mid)
        kb = k * jnp.exp(g_mid - gcum)
        q_in = (qa * jnp.exp(g_mid)).astype(BF16)
        k_st = (kb * jnp.exp(g_end - g_mid)).astype(BF16)
        qa = qa.astype(BF16)
        kb = kb.astype(BF16)
        v = i_ref[rows, :].astype(BF16)
        decay = jnp.exp(g_end)
        heads = range(n_heads)
        cs = [slice(hh * C_DIM, (hh + 1) * C_DIM) for hh in heads]
        attn = [_dot_nt(qa[:, c], kb[:, c]) for c in cs]
        state = [state_ref[hh] for hh in heads]
        inter = [_dot(q_in[:, c], s.astype(BF16)) for c, s in zip(cs, state)]
        update = [_dot_tn(k_st[:, c], v[:, c]) for c in cs]
        attn = [jnp.where(lower, a, 0.0).astype(BF16) for a in attn]
        o = [_dot(a, v[:, c]) + x for a, c, x in zip(attn, cs, inter)]
        for hh in heads:
            decay_rows = jnp.broadcast_to(decay[:, cs[hh]], (C_DIM, C_DIM)).T
            state_ref[hh] = decay_rows * state[hh] + update[hh]
        for hh in heads:
            out = _rms(o[hh], og_ref[...]) * jax.nn.silu(gate_ref[rows, cs[hh]])
            o_ref[rows, cs[hh]] = out.astype(BF16)
        return carry

    lax.fori_loop(0, n_rows // C_CHUNK, body, 0, unroll=2)


def _hgrn(z, lb_logits, out_gain, batch, seq, layer):
    t = z.shape[0]
    width = C_HEADS * C_DIM
    bw = HGRN_HEADS * C_DIM
    nb = width // bw
    rows = min(HGRN_ROWS, seq)
    ns = seq // rows
    assert seq % rows == 0 and rows % C_CHUNK == 0
    depth = lb_logits.shape[0]
    col = lambda part: pl.BlockSpec((rows, bw), lambda b, j, s, part=part: (b * ns + s, part * nb + j))
    og = out_gain.reshape(1, C_DIM)
    return pl.pallas_call(
        functools.partial(_hgrn_kernel, layer=layer),
        grid=(batch, nb, ns),
        in_specs=[col(0), col(1), col(2), col(3),
                  pl.BlockSpec((depth, bw), lambda b, j, s: (0, j)),
                  pl.BlockSpec((1, C_DIM), lambda b, j, s: (0, 0))],
        out_specs=pl.BlockSpec((rows, bw), lambda b, j, s: (b * ns + s, j)),
        out_shape=jax.ShapeDtypeStruct((t, width), BF16),
        scratch_shapes=[pltpu.VMEM((HGRN_HEADS, C_DIM, C_DIM), F32)],
        compiler_params=_params("parallel", "parallel", "arbitrary"),
        name="hgrn",
    )(z, z, z, z, lb_logits, og)


def _even_layer(h, batch, seq, norm_g, w_in, ln_g, ln_b, sgu_w, sgu_b, q_gain, k_gain,
                cmp_pos, cmp_w1, cmp_b1, cmp_w2, w_out, next_gain):
    d = h.shape[1]
    n_main = 3 * A_WIDTH + 2 * B_WIDTH + 6 * B_KV_HEADS * HEAD_DIM
    wg = w_in[:, n_main:].reshape(d, 3, B_KV_HEADS, B_GROUP).transpose(0, 2, 1, 3)
    wg = wg.reshape(d, B_KV_HEADS, 3 * B_GROUP)
    wg = jnp.pad(wg, ((0, 0), (0, 0), (0, LANES - 3 * B_GROUP))).reshape(d, B_KV_HEADS * LANES)
    n_tiled = n_main // PROJ_TN * PROJ_TN
    w_tail = jnp.concatenate([w_in[:, n_tiled:n_main], wg], axis=1)
    z, z_tail = _rms_matmul(h, norm_g, w_in[:, :n_tiled].astype(BF16), w_tail.astype(BF16))

    sgu_b_full = jnp.broadcast_to(sgu_b[:, :, None], (A_GROUPS, A_CHUNK, HEAD_DIM))
    a_out = _mixer_a(z, ln_g, ln_b, sgu_w, sgu_b_full)

    gw = B_GROUP * HEAD_DIM
    q_col0 = 3 * A_WIDTH // gw
    bg_col0 = (3 * A_WIDTH + B_WIDTH) // gw
    assert (3 * A_WIDTH + 2 * B_WIDTH) <= n_tiled and n_tiled % (B_KV_HEADS * HEAD_DIM) == 0
    kv_srcs = []
    for i in range(6):
        col = 3 * A_WIDTH + 2 * B_WIDTH + i * B_KV_HEADS * HEAD_DIM
        kv_srcs.append((z, col // HEAD_DIM) if col < n_tiled else (z_tail, (col - n_tiled) // HEAD_DIM))
    zg_col0 = (n_main - n_tiled) // LANES
    w1 = cmp_w1.reshape(2, CMP_LEN, HEAD_DIM, CMP_HIDDEN).astype(BF16)
    kc, vct, ks, vst, kw, vwt = _kv_prep(kv_srcs, batch, seq, k_gain, cmp_pos, w1,
                                         cmp_b1.reshape(2, 1, CMP_HIDDEN), cmp_w2.astype(BF16))
    b_out = _nsa_attn(z, z_tail, kc, vct, ks, vst, kw, vwt, q_gain, batch, seq, q_col0, bg_col0, zg_col0)

    w_out = w_out.astype(BF16)
    return _matmul_residual(h, [a_out, b_out], [w_out[:A_WIDTH], w_out[A_WIDTH:]], next_gain)


def _odd_layer(h, hn, batch, seq, layer, norm_g, w_in, lb_logits, out_gain, w_out, next_gain):
    w_in = w_in.astype(BF16)
    z = _matmul(hn, w_in) if hn is not None else _rms_matmul(h, norm_g, w_in)
    o = _hgrn(z, lb_logits, out_gain, batch, seq, layer)
    return _matmul_residual(h, [o], [w_out.astype(BF16)], next_gain)


def kernel(x, even_norm, even_w_in, sgu_ln_g, sgu_ln_b, sgu_w, sgu_b, nsa_q_gain, nsa_k_gain, cmp_pos, cmp_w1, cmp_b1, cmp_w2, even_w_out, odd_norm, odd_w_in, hgrn_lb, hgrn_out_gain, odd_w_out):
    batch, seq, d = x.shape
    depth = hgrn_lb.shape[0]
    h = x.reshape(batch * seq, d)
    hn = None
    for layer in range(depth):
        next_gain = odd_norm[(layer + 1) // 2] if (layer + 1 < depth and layer % 2 == 0) else None
        if layer % 2 == 0:
            e = layer // 2
            h, hn = _even_layer(h, batch, seq, even_norm[e], even_w_in[e], sgu_ln_g[e], sgu_ln_b[e],
                                sgu_w[e], sgu_b[e], nsa_q_gain[e], nsa_k_gain[e], cmp_pos[e], cmp_w1[e],
                                cmp_b1[e], cmp_w2[e], even_w_out[e], next_gain)
        else:
            o = layer // 2
            h, hn = _odd_layer(h, hn, batch, seq, layer, odd_norm[o], odd_w_in[o], hgrn_lb,
                               hgrn_out_gain[o], odd_w_out[o], next_gain)
    return h.reshape(batch, seq, d)
```

```python
import functools

import numpy as np
import jax
import jax.numpy as jnp
from jax import lax
from jax.experimental import pallas as pl
from jax.experimental.pallas import tpu as pltpu

HEAD_DIM = 128
A_GROUPS = 8
A_WIDTH = A_GROUPS * HEAD_DIM
A_CHUNK = 128
B_HEADS = 8
B_KV_HEADS = 2
B_GROUP = B_HEADS // B_KV_HEADS
B_WIDTH = B_HEADS * HEAD_DIM
CMP_LEN = 32
CMP_STRIDE = 16
CMP_HIDDEN = 256
SLC_BLOCK = 64
TOP_N = 8
WINDOW = 512
C_HEADS = 16
C_DIM = 128
C_CHUNK = 64
EPS = 1e-6
NEG_INF = -1e30
FORCE_SCORE = 1e6

LANES = 128
BF16_SUBLANES = 16
VMEM_LIMIT_BYTES = 56 * 1024 * 1024

PROJ_TM = 1024
PLAIN_TN = 2304
RESID_TM = 512
RESID_TN = 2048
NORM_ROWS = 512
MIXA_ROWS = 512
ATT_TQ = 256
VT_ROWS = HEAD_DIM + BF16_SUBLANES
KEY_COLS = 2 * HEAD_DIM
ALIBI_SPLIT = 64
EXTRA_BLOCK0 = BF16_SUBLANES
HGRN_HEADS = 8
HGRN_ROWS = 1024

BF16 = jnp.bfloat16
F32 = jnp.float32


def _params(*sem):
    return pltpu.CompilerParams(dimension_semantics=sem, vmem_limit_bytes=VMEM_LIMIT_BYTES)


def _col_tile(n, limit):
    if n <= limit:
        return n
    return max(c for c in range(LANES, limit + 1, LANES) if n % c == 0)


def _dot(a, b):
    return jnp.dot(a, b, preferred_element_type=F32)


def _dot_nt(a, b):
    return lax.dot_general(a, b, (((1,), (1,)), ((), ())), preferred_element_type=F32)


def _dot_tn(a, b):
    return lax.dot_general(a, b, (((0,), (0,)), ((), ())), preferred_element_type=F32)


def _rms(x, gain):
    return x * lax.rsqrt(jnp.mean(x * x, axis=-1, keepdims=True) + EPS) * gain


def _rms_norm_kernel(x_ref, g_ref, o_ref):
    o_ref[...] = _rms(x_ref[...], g_ref[...]).astype(BF16)


def _rms_norm(x, gain):
    t, d = x.shape
    tm = min(NORM_ROWS, t)
    assert t % tm == 0
    return pl.pallas_call(
        _rms_norm_kernel,
        grid=(t // tm,),
        in_specs=[pl.BlockSpec((tm, d), lambda i: (i, 0)), pl.BlockSpec((1, d), lambda i: (0, 0))],
        out_specs=pl.BlockSpec((tm, d), lambda i: (i, 0)),
        out_shape=jax.ShapeDtypeStruct((t, d), BF16),
        compiler_params=_params("parallel"),
        name="rms_norm",
    )(x, gain.reshape(1, d))


def _matmul_residual_kernel(x_ref, *refs, n_pairs, emit_norm):
    acts, ws, rest = refs[:n_pairs], refs[n_pairs:2 * n_pairs], refs[2 * n_pairs:]
    acc = x_ref[...]
    for a_ref, w_ref in zip(acts, ws):
        acc = acc + _dot(a_ref[...], w_ref[...])
    if emit_norm:
        gain_ref, o_ref, xn_ref = rest
        xn_ref[...] = _rms(acc, gain_ref[...]).astype(BF16)
    else:
        (o_ref,) = rest
    o_ref[...] = acc


def _matmul_residual(x, acts, ws, next_gain=None):
    t, d = x.shape
    tm = min(RESID_TM, t)
    tn = _col_tile(d, RESID_TN)
    assert t % tm == 0 and d % tn == 0
    emit_norm = next_gain is not None and tn == d
    in_specs = [pl.BlockSpec((tm, tn), lambda i, j: (i, j))]
    in_specs += [pl.BlockSpec((tm, a.shape[1]), lambda i, j: (i, 0)) for a in acts]
    in_specs += [pl.BlockSpec((w.shape[0], tn), lambda i, j: (0, j)) for w in ws]
    out_specs = [pl.BlockSpec((tm, tn), lambda i, j: (i, j))]
    out_shape = [jax.ShapeDtypeStruct((t, d), F32)]
    args = [x, *acts, *ws]
    if emit_norm:
        in_specs.append(pl.BlockSpec((1, d), lambda i, j: (0, 0)))
        out_specs.append(pl.BlockSpec((tm, d), lambda i, j: (i, 0)))
        out_shape.append(jax.ShapeDtypeStruct((t, d), BF16))
        args.append(next_gain.reshape(1, d))
    out = pl.pallas_call(
        functools.partial(_matmul_residual_kernel, n_pairs=len(acts), emit_norm=emit_norm),
        grid=(t // tm, d // tn),
        in_specs=in_specs,
        out_specs=out_specs,
        out_shape=out_shape,
        compiler_params=_params("parallel", "arbitrary"),
        name="matmul_residual",
    )(*args)
    return (out[0], out[1]) if emit_norm else (out[0], None)


def _matmul_kernel(x_ref, w_ref, o_ref):
    o_ref[...] = _dot(x_ref[...], w_ref[...])


def _matmul(xn, w):
    t, d = xn.shape
    n = w.shape[1]
    tm = min(PROJ_TM, t)
    tn = _col_tile(n, PLAIN_TN)
    assert t % tm == 0 and n % tn == 0
    return pl.pallas_call(
        _matmul_kernel,
        grid=(t // tm, n // tn),
        in_specs=[pl.BlockSpec((tm, d), lambda i, j: (i, 0)), pl.BlockSpec((d, tn), lambda i, j: (0, j))],
        out_specs=pl.BlockSpec((tm, tn), lambda i, j: (i, j)),
        out_shape=jax.ShapeDtypeStruct((t, n), F32),
        compiler_params=_params("parallel", "arbitrary"),
        name="matmul",
    )(xn, w)


def _mixer_a_kernel(u_ref, v_ref, gate_ref, lng_ref, lnb_ref, w_ref, b_ref, o_ref):
    rows = u_ref.shape[0]
    v = jax.nn.gelu(v_ref[...])
    mu = jnp.mean(v, axis=-1, keepdims=True)
    vc = v - mu
    var = jnp.mean(vc * vc, axis=-1, keepdims=True)
    vn = (vc * lax.rsqrt(var + EPS) * lng_ref[...] + lnb_ref[...]).astype(BF16)
    tr = lax.broadcasted_iota(jnp.int32, (A_CHUNK, A_CHUNK), 0)
    tc = lax.broadcasted_iota(jnp.int32, (A_CHUNK, A_CHUNK), 1)
    for g in range(A_GROUPS):
        cols = slice(g * HEAD_DIM, (g + 1) * HEAD_DIM)
        wm = jnp.where(tr >= tc, w_ref[g], 0.0).astype(BF16)
        for c in range(rows // A_CHUNK):
            rs = slice(c * A_CHUNK, (c + 1) * A_CHUNK)
            sv = _dot(wm, vn[rs, cols]) + b_ref[g]
            u = jax.nn.gelu(u_ref[rs, cols])
            o_ref[rs, cols] = (u * sv * jax.nn.silu(gate_ref[rs, cols])).astype(BF16)


def _mixer_a(z, ln_g, ln_b, sgu_w, sgu_b_full):
    t = z.shape[0]
    rows = min(MIXA_ROWS, t)
    assert t % rows == 0 and rows % A_CHUNK == 0
    blk = lambda c: pl.BlockSpec((rows, A_WIDTH), lambda i: (i, c))
    full = lambda a: pl.BlockSpec(a.shape, lambda i: (0,) * a.ndim)
    lng = ln_g.reshape(1, A_WIDTH)
    lnb = ln_b.reshape(1, A_WIDTH)
    return pl.pallas_call(
        _mixer_a_kernel,
        grid=(t // rows,),
        in_specs=[blk(0), blk(1), blk(2), full(lng), full(lnb), full(sgu_w), full(sgu_b_full)],
        out_specs=pl.BlockSpec((rows, A_WIDTH), lambda i: (i, 0)),
        out_shape=jax.ShapeDtypeStruct((t, A_WIDTH), BF16),
        compiler_params=_params("parallel"),
        name="mixer_a",
    )(z, z, z, lng, lnb, sgu_w, sgu_b_full)


def _compress(src_ref, pos_ref, w1_ref, b1_ref, w2_ref):
    groups = src_ref.shape[0] // CMP_STRIDE
    lo = jnp.zeros((groups, CMP_HIDDEN), F32)
    hi = jnp.zeros((groups, CMP_HIDDEN), F32)
    for l in range(CMP_STRIDE):
        xl = src_ref[pl.ds(l, groups, stride=CMP_STRIDE), :]
        lo = lo + _dot((xl + pos_ref[l:l + 1, :]).astype(BF16), w1_ref[l])
        hi = hi + _dot((xl + pos_ref[CMP_STRIDE + l:CMP_STRIDE + l + 1, :]).astype(BF16),
                       w1_ref[CMP_STRIDE + l])
    hid = jax.nn.gelu(lo + pltpu.roll(hi, groups - 1, 0) + b1_ref[...])
    return _dot(hid.astype(BF16), w2_ref[...])


def _store_values_t(v_src, vt_ref):
    n_tiles, rows, tk = vt_ref.shape
    for kt in range(n_tiles):
        vt_ref[kt, :HEAD_DIM, :] = v_src[kt * tk:(kt + 1) * tk, :].T.astype(BF16)
        vt_ref[kt, HEAD_DIM:, :] = jnp.ones((rows - HEAD_DIM, tk), BF16)


def _key_extras(seq, with_blocks):
    kpos = lax.broadcasted_iota(jnp.int32, (seq, LANES), 0)
    c = lax.broadcasted_iota(jnp.int32, (seq, LANES), 1)
    hi = (kpos // ALIBI_SPLIT).astype(F32)
    lo = (kpos % ALIBI_SPLIT).astype(F32)
    ex = jnp.where(c == 0, hi, jnp.where(c == 1, lo, 0.0))
    if with_blocks:
        ex = jnp.where(kpos // SLC_BLOCK == c - EXTRA_BLOCK0, 1.0, ex)
    return ex.astype(BF16)


def _kv_prep_kernel(kc_src, vc_src, ks_src, vs_src, kw_src, vw_src, kgain_ref, pos_ref, w1_ref,
                    b1_ref, w2_ref, kc_ref, vct_ref, ks_ref, vst_ref, kw_ref, vwt_ref):
    seq = ks_src.shape[0]
    ks_ref[:, :HEAD_DIM] = _rms(ks_src[...], kgain_ref[1:2, :]).astype(BF16)
    ks_ref[:, HEAD_DIM:] = _key_extras(seq, True)
    kw_ref[:, :HEAD_DIM] = _rms(kw_src[...], kgain_ref[2:3, :]).astype(BF16)
    kw_ref[:, HEAD_DIM:] = _key_extras(seq, False)
    _store_values_t(vs_src, vst_ref)
    _store_values_t(vw_src, vwt_ref)
    kc = _compress(kc_src, pos_ref.at[0], w1_ref.at[0], b1_ref.at[0], w2_ref.at[0])
    kc_ref[...] = _rms(kc, kgain_ref[0:1, :]).astype(BF16)
    vc = _compress(vc_src, pos_ref.at[1], w1_ref.at[1], b1_ref.at[1], w2_ref.at[1])
    vct_ref[...] = vc.T.astype(BF16)


def _kv_prep(kv_srcs, batch, seq, k_gain, cmp_pos, w1, b1, w2):
    groups = seq // CMP_STRIDE
    tk = min(ATT_TQ, seq)
    src = lambda i: pl.BlockSpec((seq, HEAD_DIM), lambda b, h, c0=kv_srcs[i][1]: (b, c0 + h))
    full = lambda a: pl.BlockSpec(a.shape, lambda b, h: (0,) * a.ndim)
    spec4 = lambda r, c: pl.BlockSpec((None, None, r, c), lambda b, h: (b, h, 0, 0))
    shape4 = lambda r, c: jax.ShapeDtypeStruct((batch, B_KV_HEADS, r, c), BF16)
    vt_spec = pl.BlockSpec((None, None, seq // tk, VT_ROWS, tk), lambda b, h: (b, h, 0, 0, 0))
    vt_shape = jax.ShapeDtypeStruct((batch, B_KV_HEADS, seq // tk, VT_ROWS, tk), BF16)
    return pl.pallas_call(
        _kv_prep_kernel,
        grid=(batch, B_KV_HEADS),
        in_specs=[src(i) for i in range(6)] + [full(k_gain), full(cmp_pos), full(w1), full(b1), full(w2)],
        out_specs=[spec4(groups, HEAD_DIM), spec4(HEAD_DIM, groups), spec4(seq, KEY_COLS), vt_spec,
                   spec4(seq, KEY_COLS), vt_spec],
        out_shape=[shape4(groups, HEAD_DIM), shape4(HEAD_DIM, groups), shape4(seq, KEY_COLS), vt_shape,
                   shape4(seq, KEY_COLS), vt_shape],
        compiler_params=_params("parallel", "parallel"),
        name="kv_prep",
    )(*[a for a, _ in kv_srcs], k_gain, cmp_pos, w1, b1, w2)


def _nsa_kernel(slopes_ref, q_ref, bg_ref, gl_ref, kc_ref, vct_ref, ks_ref, vst_ref, kw_ref, vwt_ref,
                qgain_ref, ovt_ref, o_ref, qt_ref, m_ref, acc_ref, *, n_cmp):
    tq = q_ref.shape[0]
    n_grp = kc_ref.shape[0]
    n_slc = ovt_ref.shape[0]
    h = pl.program_id(1)
    i = pl.program_id(2)
    t0 = i * tq

    for g in range(B_GROUP):
        q = q_ref[:, g * HEAD_DIM:(g + 1) * HEAD_DIM]
        qt_ref[g, :HEAD_DIM, :] = (_rms(q, qgain_ref[...]) * (HEAD_DIM ** -0.5)).T.astype(BF16)

    n_idx = lax.broadcasted_iota(jnp.int32, (n_grp, tq), 0)
    t_idx = t0 + lax.broadcasted_iota(jnp.int32, (n_grp, tq), 1)
    dist_c = t_idx - (n_idx * CMP_STRIDE + (CMP_LEN - 1))
    valid_c = (n_idx < n_cmp) & (dist_c >= 0)
    dist_cf = dist_c.astype(F32)
    groups = range(B_GROUP)
    s_c = [_dot(kc_ref[...], qt_ref[g, :HEAD_DIM, :]) for g in groups]
    s_c = [jnp.where(valid_c, s - slopes_ref[h * B_GROUP + g] * dist_cf, NEG_INF) for g, s in zip(groups, s_c)]
    m_c = [jnp.max(s, axis=0, keepdims=True) for s in s_c]
    e_c = [jnp.where(valid_c, jnp.exp(s - m), 0.0) for s, m in zip(s_c, m_c)]
    p_c = [e / jnp.maximum(jnp.sum(e, axis=0, keepdims=True), 1e-30) for e in e_c]
    o_cmp = [_dot(vct_ref[...], p.astype(BF16)) for p in p_c]
    p_sum = sum(p_c[1:], p_c[0])

    imp = jnp.dot(ovt_ref[...], p_sum, precision=lax.Precision.HIGHEST, preferred_element_type=F32)
    j_idx = lax.broadcasted_iota(jnp.int32, (n_slc, tq), 0)
    j_f = j_idx.astype(F32)
    t_sel = t0 + lax.broadcasted_iota(jnp.int32, (n_slc, tq), 1)
    cur = t_sel // SLC_BLOCK
    forced = (j_idx == 0) | (j_idx == cur) | (j_idx == cur - 1)
    val = jnp.where(forced, FORCE_SCORE, jnp.where(j_idx * SLC_BLOCK <= t_sel, imp, -1.0))
    sel_t = jnp.zeros((n_slc, tq), F32)
    for _ in range(min(TOP_N, n_slc)):
        best = jnp.max(val, axis=0, keepdims=True)
        first = jnp.min(jnp.where(val == best, j_f, float(n_slc)), axis=0, keepdims=True)
        hit = j_f == first
        sel_t = jnp.where(hit, 1.0, sel_t)
        val = jnp.where(hit, -jnp.inf, val)

    unselected = (sel_t - 1.0) * (-NEG_INF)
    e_row = lax.broadcasted_iota(jnp.int32, (EXTRA_BLOCK0, tq), 0)
    pad = jnp.zeros((HEAD_DIM - EXTRA_BLOCK0 - n_slc, tq), F32)
    for g in range(B_GROUP):
        slope = slopes_ref[h * B_GROUP + g]
        alibi = jnp.where(e_row == 0, slope * ALIBI_SPLIT, jnp.where(e_row == 1, slope, 0.0))
        qt_ref[g, HEAD_DIM:, :] = jnp.concatenate([alibi, unselected, pad], axis=0).astype(BF16)

    key_r = lax.broadcasted_iota(jnp.int32, (tq, tq), 0)
    qry_c = lax.broadcasted_iota(jnp.int32, (tq, tq), 1)
    causal = key_r <= qry_c

    SLC, WIN = 0, 1
    branch_refs = {SLC: (ks_ref, vst_ref), WIN: (kw_ref, vwt_ref)}

    def tiles_update(tiles, first):
        heads = range(B_GROUP)
        s = []
        for br, kt, mask in tiles:
            k_blk = branch_refs[br][0][pl.ds(pl.multiple_of(kt * tq, tq), tq), :]
            s.append([_dot(k_blk, qt_ref[g]) for g in heads])
        s = [[sg if mask is None else jnp.where(mask, sg, NEG_INF) for sg in st]
             for st, (_, _, mask) in zip(s, tiles)]
        branches = sorted({br for br, _, _ in tiles})
        m_old, m_new = {}, {}
        for br in branches:
            for g in heads:
                m = None
                for st, (tb, _, _) in zip(s, tiles):
                    if tb == br:
                        tile_max = jnp.max(st[g], axis=0, keepdims=True)
                        m = tile_max if m is None else jnp.maximum(m, tile_max)
                if not first:
                    m_old[br, g] = m_ref[br * B_GROUP + g]
                    m = jnp.maximum(m_old[br, g], m)
                m_new[br, g] = m
        p = [[jnp.exp(st[g] - m_new[br, g]).astype(BF16) for g in heads] for st, (br, _, _) in zip(s, tiles)]
        pv = [[_dot(branch_refs[br][1][kt], pt[g]) for g in heads] for pt, (br, kt, _) in zip(p, tiles)]
        for br in branches:
            for g in heads:
                total = None
                for pvt, (tb, _, _) in zip(pv, tiles):
                    if tb == br:
                        total = pvt[g] if total is None else total + pvt[g]
                slot = br * B_GROUP + g
                if first:
                    acc_ref[slot] = total
                else:
                    acc_ref[slot] = jnp.exp(m_old[br, g] - m_new[br, g]) * acc_ref[slot] + total
                m_ref[slot] = m_new[br, g]

    def finish(br):
        return [acc_ref[br * B_GROUP + g, :HEAD_DIM, :] / acc_ref[br * B_GROUP + g, HEAD_DIM:HEAD_DIM + 1, :]
                for g in range(B_GROUP)]

    tiles_update([(SLC, i, causal), (WIN, i, causal)], True)

    def slc_pair(j, carry):
        tiles_update([(SLC, 2 * j, None), (SLC, 2 * j + 1, None)], False)
        return carry

    lax.fori_loop(0, i // 2, slc_pair, 0)

    @pl.when(i % 2 == 1)
    def _():
        tiles_update([(SLC, i - 1, None)], False)

    win_tiles = (WINDOW + tq - 1) // tq

    def win_tile(off):
        needs_mask = off * tq + tq - 1 >= WINDOW
        return (WIN, i - off, ((qry_c - key_r) + off * tq < WINDOW) if needs_mask else None)

    for count in range(1, win_tiles + 1):
        @pl.when(jnp.minimum(i, win_tiles) == count)
        def _(count=count):
            tiles_update([win_tile(off) for off in range(1, count + 1)], False)

    o_slc = finish(SLC)
    o_win = finish(WIN)

    gates_t = jax.nn.sigmoid(gl_ref[...].T)
    for g in range(B_GROUP):
        c = slice(g * HEAD_DIM, (g + 1) * HEAD_DIM)
        o_t = (gates_t[g:g + 1] * o_cmp[g]
               + gates_t[B_GROUP + g:B_GROUP + g + 1] * o_slc[g]
               + gates_t[2 * B_GROUP + g:2 * B_GROUP + g + 1] * o_win[g])
        o_ref[:, c] = (o_t.T * jax.nn.silu(bg_ref[:, c])).astype(BF16)


def _overlap_t(n_grp, n_cmp, n_slc):
    cs = np.arange(n_grp)[None, :] * CMP_STRIDE
    ss = np.arange(n_slc)[:, None] * SLC_BLOCK
    ov = np.clip(np.minimum(cs + CMP_LEN, ss + SLC_BLOCK) - np.maximum(cs, ss), 0, None) / CMP_LEN
    ov = ov * (np.arange(n_grp)[None, :] < n_cmp)
    return jnp.asarray(ov, dtype=F32)


def _nsa_attn(z, zg, kc, vct, ks, vst, kw, vwt, q_gain, batch, seq, q_col0, bg_col0, zg_col0):
    t = z.shape[0]
    tq = min(ATT_TQ, seq)
    assert seq % tq == 0 and tq % SLC_BLOCK == 0 and seq // SLC_BLOCK <= LANES
    nq = seq // tq
    n_grp = seq // CMP_STRIDE
    n_cmp = (seq - CMP_LEN) // CMP_STRIDE + 1
    n_slc = seq // SLC_BLOCK
    gw = B_GROUP * HEAD_DIM
    h_idx = jnp.arange(1, B_HEADS + 1, dtype=F32)
    slopes = jnp.exp2(-8.0 * h_idx / B_HEADS)
    ovt = _overlap_t(n_grp, n_cmp, n_slc)
    rows = lambda b, h, i: b * nq + i
    spec4 = lambda r, c: pl.BlockSpec((None, None, r, c), lambda b, h, i: (b, h, 0, 0))
    vt_spec = pl.BlockSpec((None, None, nq, VT_ROWS, tq), lambda b, h, i: (b, h, 0, 0, 0))
    full = lambda a: pl.BlockSpec(a.shape, lambda b, h, i: (0,) * a.ndim)
    qg = q_gain.reshape(1, HEAD_DIM)
    return pl.pallas_call(
        functools.partial(_nsa_kernel, n_cmp=n_cmp),
        grid=(batch, B_KV_HEADS, nq),
        in_specs=[
            pl.BlockSpec(memory_space=pltpu.SMEM),
            pl.BlockSpec((tq, gw), lambda b, h, i: (rows(b, h, i), q_col0 + h)),
            pl.BlockSpec((tq, gw), lambda b, h, i: (rows(b, h, i), bg_col0 + h)),
            pl.BlockSpec((tq, LANES), lambda b, h, i: (rows(b, h, i), zg_col0 + h)),
            spec4(n_grp, HEAD_DIM), spec4(HEAD_DIM, n_grp), spec4(seq, KEY_COLS), vt_spec,
            spec4(seq, KEY_COLS), vt_spec, full(qg), full(ovt),
        ],
        out_specs=pl.BlockSpec((tq, gw), lambda b, h, i: (rows(b, h, i), h)),
        out_shape=jax.ShapeDtypeStruct((t, B_WIDTH), BF16),
        scratch_shapes=[
            pltpu.VMEM((B_GROUP, KEY_COLS, tq), BF16),
            pltpu.VMEM((2 * B_GROUP, 1, tq), F32),
            pltpu.VMEM((2 * B_GROUP, VT_ROWS, tq), F32),
        ],
        compiler_params=_params("parallel", "parallel", "arbitrary"),
        name="nsa_attn",
    )(slopes, z, z, zg, kc, vct, ks, vst, kw, vwt, qg, ovt)


def _hgrn_kernel(q_ref, f_ref, i_ref, gate_ref, lb_ref, og_ref, o_ref, state_ref, *, layer):
    n_rows = q_ref.shape[0]
    n_heads = q_ref.shape[1] // C_DIM
    mid = C_CHUNK // 2 - 1

    logits = lb_ref[...]
    e = jnp.exp(logits - jnp.max(logits, axis=0, keepdims=True))
    p = e / jnp.sum(e, axis=0, keepdims=True)
    lb = jnp.sum(p[:layer + 1], axis=0, keepdims=True) - p[0:1]

    r_i = lax.broadcasted_iota(jnp.int32, (C_CHUNK, C_CHUNK), 0)
    c_i = lax.broadcasted_iota(jnp.int32, (C_CHUNK, C_CHUNK), 1)
    lower = r_i >= c_i
    tri = jnp.where(lower, 1.0, 0.0).astype(BF16)

    @pl.when(pl.program_id(2) == 0)
    def _():
        state_ref[...] = jnp.zeros_like(state_ref)

    def body(c, carry):
        r0 = pl.multiple_of(c * C_CHUNK, C_CHUNK)
        rows = pl.ds(r0, C_CHUNK)
        f = lb + (1.0 - lb) * jax.nn.sigmoid(f_ref[rows, :])
        k = 1.0 - f
        g = jnp.log(f)
        g_hi = g.astype(BF16)
        g_rest = g - g_hi.astype(F32)
        g_mid = g_rest.astype(BF16)
        g_lo = (g_rest - g_mid.astype(F32)).astype(BF16)
        gcum = _dot(tri, g_hi) + _dot(tri, g_mid) + _dot(tri, g_lo)
        g_mid = gcum[mid:mid + 1, :]
        g_end = gcum[C_CHUNK - 1:C_CHUNK, :]
        qa = q_ref[rows, :] * jnp.exp(gcum - g_mid)
        kb = k * jnp.exp(g_mid - gcum)
        q_in = (qa * jnp.exp(g_mid)).astype(BF16)
        k_st = (kb * jnp.exp(g_end - g_mid)).astype(BF16)
        qa = qa.astype(BF16)
        kb = kb.astype(BF16)
        v = i_ref[rows, :].astype(BF16)
        decay = jnp.exp(g_end)
        heads = range(n_heads)
        cs = [slice(hh * C_DIM, (hh + 1) * C_DIM) for hh in heads]
        attn = [_dot_nt(qa[:, c], kb[:, c]) for c in cs]
        state = [state_ref[hh] for hh in heads]
        inter = [_dot(q_in[:, c], s.astype(BF16)) for c, s in zip(cs, state)]
        update = [_dot_tn(k_st[:, c], v[:, c]) for c in cs]
        attn = [jnp.where(lower, a, 0.0).astype(BF16) for a in attn]
        o = [_dot(a, v[:, c]) + x for a, c, x in zip(attn, cs, inter)]
        for hh in heads:
            decay_rows = jnp.broadcast_to(decay[:, cs[hh]], (C_DIM, C_DIM)).T
            state_ref[hh] = decay_rows * state[hh] + update[hh]
        for hh in heads:
            out = _rms(o[hh], og_ref[...]) * jax.nn.silu(gate_ref[rows, cs[hh]])
            o_ref[rows, cs[hh]] = out.astype(BF16)
        return carry

    lax.fori_loop(0, n_rows // C_CHUNK, body, 0, unroll=2)


def _hgrn(z, lb_logits, out_gain, batch, seq, layer):
    t = z.shape[0]
    width = C_HEADS * C_DIM
    bw = HGRN_HEADS * C_DIM
    nb = width // bw
    rows = min(HGRN_ROWS, seq)
    ns = seq // rows
    assert seq % rows == 0 and rows % C_CHUNK == 0
    depth = lb_logits.shape[0]
    col = lambda part: pl.BlockSpec((rows, bw), lambda b, j, s, part=part: (b * ns + s, part * nb + j))
    og = out_gain.reshape(1, C_DIM)
    return pl.pallas_call(
        functools.partial(_hgrn_kernel, layer=layer),
        grid=(batch, nb, ns),
        in_specs=[col(0), col(1), col(2), col(3),
                  pl.BlockSpec((depth, bw), lambda b, j, s: (0, j)),
                  pl.BlockSpec((1, C_DIM), lambda b, j, s: (0, 0))],
        out_specs=pl.BlockSpec((rows, bw), lambda b, j, s: (b * ns + s, j)),
        out_shape=jax.ShapeDtypeStruct((t, width), BF16),
        scratch_shapes=[pltpu.VMEM((HGRN_HEADS, C_DIM, C_DIM), F32)],
        compiler_params=_params("parallel", "parallel", "arbitrary"),
        name="hgrn",
    )(z, z, z, z, lb_logits, og)


def _even_layer(h, batch, seq, norm_g, w_in, ln_g, ln_b, sgu_w, sgu_b, q_gain, k_gain,
                cmp_pos, cmp_w1, cmp_b1, cmp_w2, w_out, next_gain):
    d = h.shape[1]
    n_main = 3 * A_WIDTH + 2 * B_WIDTH + 6 * B_KV_HEADS * HEAD_DIM
    wg = w_in[:, n_main:].reshape(d, 3, B_KV_HEADS, B_GROUP).transpose(0, 2, 1, 3)
    wg = wg.reshape(d, B_KV_HEADS, 3 * B_GROUP)
    wg = jnp.pad(wg, ((0, 0), (0, 0), (0, LANES - 3 * B_GROUP))).reshape(d, B_KV_HEADS * LANES)
    w_all = jnp.concatenate([w_in[:, :n_main], wg], axis=1).astype(BF16)
    z = _matmul(_rms_norm(h, norm_g), w_all)

    sgu_b_full = jnp.broadcast_to(sgu_b[:, :, None], (A_GROUPS, A_CHUNK, HEAD_DIM))
    a_out = _mixer_a(z, ln_g, ln_b, sgu_w, sgu_b_full)

    gw = B_GROUP * HEAD_DIM
    q_col0 = 3 * A_WIDTH // gw
    bg_col0 = (3 * A_WIDTH + B_WIDTH) // gw
    kv_col0 = (3 * A_WIDTH + 2 * B_WIDTH) // HEAD_DIM
    kv_srcs = [(z, kv_col0 + i * B_KV_HEADS) for i in range(6)]
    zg_col0 = n_main // LANES
    w1 = cmp_w1.reshape(2, CMP_LEN, HEAD_DIM, CMP_HIDDEN).astype(BF16)
    kc, vct, ks, vst, kw, vwt = _kv_prep(kv_srcs, batch, seq, k_gain, cmp_pos, w1,
                                         cmp_b1.reshape(2, 1, CMP_HIDDEN), cmp_w2.astype(BF16))
    b_out = _nsa_attn(z, z, kc, vct, ks, vst, kw, vwt, q_gain, batch, seq, q_col0, bg_col0, zg_col0)

    w_out = w_out.astype(BF16)
    return _matmul_residual(h, [a_out, b_out], [w_out[:A_WIDTH], w_out[A_WIDTH:]], next_gain)


def _odd_layer(h, hn, batch, seq, layer, norm_g, w_in, lb_logits, out_gain, w_out, next_gain):
    w_in = w_in.astype(BF16)
    z = _matmul(hn if hn is not None else _rms_norm(h, norm_g), w_in)
    o = _hgrn(z, lb_logits, out_gain, batch, seq, layer)
    return _matmul_residual(h, [o], [w_out.astype(BF16)], next_gain)


def kernel(x, even_norm, even_w_in, sgu_ln_g, sgu_ln_b, sgu_w, sgu_b, nsa_q_gain, nsa_k_gain, cmp_pos, cmp_w1, cmp_b1, cmp_w2, even_w_out, odd_norm, odd_w_in, hgrn_lb, hgrn_out_gain, odd_w_out):
    batch, seq, d = x.shape
    depth = hgrn_lb.shape[0]
    h = x.reshape(batch * seq, d)
    hn = None
    for layer in range(depth):
        next_gain = odd_norm[(layer + 1) // 2] if (layer + 1 < depth and layer % 2 == 0) else None
        if layer % 2 == 0:
            e = layer // 2
            h, hn = _even_layer(h, batch, seq, even_norm[e], even_w_in[e], sgu_ln_g[e], sgu_ln_b[e],
                                sgu_w[e], sgu_b[e], nsa_q_gain[e], nsa_k_gain[e], cmp_pos[e], cmp_w1[e],
                                cmp_b1[e], cmp_w2[e], even_w_out[e], next_gain)
        else:
            o = layer // 2
            h, hn = _odd_layer(h, hn, batch, seq, layer, odd_norm[o], odd_w_in[o], hgrn_lb,
                               hgrn_out_gain[o], odd_w_out[o], next_gain)
    return h.reshape(batch, seq, d)
```

```python
import functools

import numpy as np
import jax
import jax.numpy as jnp
from jax import lax
from jax.experimental import pallas as pl
from jax.experimental.pallas import tpu as pltpu

HEAD_DIM = 128
A_GROUPS = 8
A_WIDTH = A_GROUPS * HEAD_DIM
A_CHUNK = 128
B_HEADS = 8
B_KV_HEADS = 2
B_GROUP = B_HEADS // B_KV_HEADS
B_WIDTH = B_HEADS * HEAD_DIM
CMP_LEN = 32
CMP_STRIDE = 16
CMP_HIDDEN = 256
SLC_BLOCK = 64
TOP_N = 8
WINDOW = 512
C_HEADS = 16
C_DIM = 128
C_CHUNK = 64
EPS = 1e-6
NEG_INF = -1e30
FORCE_SCORE = 1e6

LANES = 128
BF16_SUBLANES = 16
VMEM_LIMIT_BYTES = 56 * 1024 * 1024

PROJ_TM = 1024
PLAIN_TN = 2304
RESID_TM = 512
RESID_TN = 2048
NORM_ROWS = 512
MIXA_ROWS = 512
ATT_TQ = 256
SLC_GROUP = 4
VT_ROWS = HEAD_DIM + BF16_SUBLANES
KEY_COLS = 2 * HEAD_DIM
ALIBI_SPLIT = 64
EXTRA_BLOCK0 = BF16_SUBLANES
HGRN_HEADS = 8
HGRN_ROWS = 1024

BF16 = jnp.bfloat16
F32 = jnp.float32


def _params(*sem):
    return pltpu.CompilerParams(dimension_semantics=sem, vmem_limit_bytes=VMEM_LIMIT_BYTES)


def _col_tile(n, limit):
    if n <= limit:
        return n
    return max(c for c in range(LANES, limit + 1, LANES) if n % c == 0)


def _dot(a, b):
    return jnp.dot(a, b, preferred_element_type=F32)


def _dot_nt(a, b):
    return lax.dot_general(a, b, (((1,), (1,)), ((), ())), preferred_element_type=F32)


def _dot_tn(a, b):
    return lax.dot_general(a, b, (((0,), (0,)), ((), ())), preferred_element_type=F32)


def _rms(x, gain):
    return x * lax.rsqrt(jnp.mean(x * x, axis=-1, keepdims=True) + EPS) * gain


def _rms_norm_kernel(x_ref, g_ref, o_ref):
    o_ref[...] = _rms(x_ref[...], g_ref[...]).astype(BF16)


def _rms_norm(x, gain):
    t, d = x.shape
    tm = min(NORM_ROWS, t)
    assert t % tm == 0
    return pl.pallas_call(
        _rms_norm_kernel,
        grid=(t // tm,),
        in_specs=[pl.BlockSpec((tm, d), lambda i: (i, 0)), pl.BlockSpec((1, d), lambda i: (0, 0))],
        out_specs=pl.BlockSpec((tm, d), lambda i: (i, 0)),
        out_shape=jax.ShapeDtypeStruct((t, d), BF16),
        compiler_params=_params("parallel"),
        name="rms_norm",
    )(x, gain.reshape(1, d))


def _matmul_residual_kernel(x_ref, *refs, n_pairs, emit_norm):
    acts, ws, rest = refs[:n_pairs], refs[n_pairs:2 * n_pairs], refs[2 * n_pairs:]
    acc = x_ref[...]
    for a_ref, w_ref in zip(acts, ws):
        acc = acc + _dot(a_ref[...], w_ref[...])
    if emit_norm:
        gain_ref, o_ref, xn_ref = rest
        xn_ref[...] = _rms(acc, gain_ref[...]).astype(BF16)
    else:
        (o_ref,) = rest
    o_ref[...] = acc


def _matmul_residual(x, acts, ws, next_gain=None):
    t, d = x.shape
    tm = min(RESID_TM, t)
    tn = _col_tile(d, RESID_TN)
    assert t % tm == 0 and d % tn == 0
    emit_norm = next_gain is not None and tn == d
    in_specs = [pl.BlockSpec((tm, tn), lambda i, j: (i, j))]
    in_specs += [pl.BlockSpec((tm, a.shape[1]), lambda i, j: (i, 0)) for a in acts]
    in_specs += [pl.BlockSpec((w.shape[0], tn), lambda i, j: (0, j)) for w in ws]
    out_specs = [pl.BlockSpec((tm, tn), lambda i, j: (i, j))]
    out_shape = [jax.ShapeDtypeStruct((t, d), F32)]
    args = [x, *acts, *ws]
    if emit_norm:
        in_specs.append(pl.BlockSpec((1, d), lambda i, j: (0, 0)))
        out_specs.append(pl.BlockSpec((tm, d), lambda i, j: (i, 0)))
        out_shape.append(jax.ShapeDtypeStruct((t, d), BF16))
        args.append(next_gain.reshape(1, d))
    out = pl.pallas_call(
        functools.partial(_matmul_residual_kernel, n_pairs=len(acts), emit_norm=emit_norm),
        grid=(t // tm, d // tn),
        in_specs=in_specs,
        out_specs=out_specs,
        out_shape=out_shape,
        compiler_params=_params("parallel", "arbitrary"),
        name="matmul_residual",
    )(*args)
    return (out[0], out[1]) if emit_norm else (out[0], None)


def _matmul_kernel(x_ref, w_ref, o_ref):
    o_ref[...] = _dot(x_ref[...], w_ref[...])


def _matmul(xn, w):
    t, d = xn.shape
    n = w.shape[1]
    tm = min(PROJ_TM, t)
    tn = _col_tile(n, PLAIN_TN)
    assert t % tm == 0 and n % tn == 0
    return pl.pallas_call(
        _matmul_kernel,
        grid=(t // tm, n // tn),
        in_specs=[pl.BlockSpec((tm, d), lambda i, j: (i, 0)), pl.BlockSpec((d, tn), lambda i, j: (0, j))],
        out_specs=pl.BlockSpec((tm, tn), lambda i, j: (i, j)),
        out_shape=jax.ShapeDtypeStruct((t, n), F32),
        compiler_params=_params("parallel", "arbitrary"),
        name="matmul",
    )(xn, w)


def _mixer_a_kernel(u_ref, v_ref, gate_ref, lng_ref, lnb_ref, w_ref, b_ref, o_ref):
    rows = u_ref.shape[0]
    v = jax.nn.gelu(v_ref[...])
    mu = jnp.mean(v, axis=-1, keepdims=True)
    vc = v - mu
    var = jnp.mean(vc * vc, axis=-1, keepdims=True)
    vn = (vc * lax.rsqrt(var + EPS) * lng_ref[...] + lnb_ref[...]).astype(BF16)
    tr = lax.broadcasted_iota(jnp.int32, (A_CHUNK, A_CHUNK), 0)
    tc = lax.broadcasted_iota(jnp.int32, (A_CHUNK, A_CHUNK), 1)
    for g in range(A_GROUPS):
        cols = slice(g * HEAD_DIM, (g + 1) * HEAD_DIM)
        wm = jnp.where(tr >= tc, w_ref[g], 0.0).astype(BF16)
        for c in range(rows // A_CHUNK):
            rs = slice(c * A_CHUNK, (c + 1) * A_CHUNK)
            sv = _dot(wm, vn[rs, cols]) + b_ref[g]
            u = jax.nn.gelu(u_ref[rs, cols])
            o_ref[rs, cols] = (u * sv * jax.nn.silu(gate_ref[rs, cols])).astype(BF16)


def _mixer_a(z, ln_g, ln_b, sgu_w, sgu_b_full):
    t = z.shape[0]
    rows = min(MIXA_ROWS, t)
    assert t % rows == 0 and rows % A_CHUNK == 0
    blk = lambda c: pl.BlockSpec((rows, A_WIDTH), lambda i: (i, c))
    full = lambda a: pl.BlockSpec(a.shape, lambda i: (0,) * a.ndim)
    lng = ln_g.reshape(1, A_WIDTH)
    lnb = ln_b.reshape(1, A_WIDTH)
    return pl.pallas_call(
        _mixer_a_kernel,
        grid=(t // rows,),
        in_specs=[blk(0), blk(1), blk(2), full(lng), full(lnb), full(sgu_w), full(sgu_b_full)],
        out_specs=pl.BlockSpec((rows, A_WIDTH), lambda i: (i, 0)),
        out_shape=jax.ShapeDtypeStruct((t, A_WIDTH), BF16),
        compiler_params=_params("parallel"),
        name="mixer_a",
    )(z, z, z, lng, lnb, sgu_w, sgu_b_full)


def _compress(src_ref, pos_ref, w1_ref, b1_ref, w2_ref):
    groups = src_ref.shape[0] // CMP_STRIDE
    lo = jnp.zeros((groups, CMP_HIDDEN), F32)
    hi = jnp.zeros((groups, CMP_HIDDEN), F32)
    for l in range(CMP_STRIDE):
        xl = src_ref[pl.ds(l, groups, stride=CMP_STRIDE), :]
        lo = lo + _dot((xl + pos_ref[l:l + 1, :]).astype(BF16), w1_ref[l])
        hi = hi + _dot((xl + pos_ref[CMP_STRIDE + l:CMP_STRIDE + l + 1, :]).astype(BF16),
                       w1_ref[CMP_STRIDE + l])
    hid = jax.nn.gelu(lo + pltpu.roll(hi, groups - 1, 0) + b1_ref[...])
    return _dot(hid.astype(BF16), w2_ref[...])


def _store_values_t(v_src, vt_ref):
    n_tiles, rows, tk = vt_ref.shape
    for kt in range(n_tiles):
        vt_ref[kt, :HEAD_DIM, :] = v_src[kt * tk:(kt + 1) * tk, :].T.astype(BF16)
        vt_ref[kt, HEAD_DIM:, :] = jnp.ones((rows - HEAD_DIM, tk), BF16)


def _key_extras(seq, with_blocks):
    kpos = lax.broadcasted_iota(jnp.int32, (seq, LANES), 0)
    c = lax.broadcasted_iota(jnp.int32, (seq, LANES), 1)
    hi = (kpos // ALIBI_SPLIT).astype(F32)
    lo = (kpos % ALIBI_SPLIT).astype(F32)
    ex = jnp.where(c == 0, hi, jnp.where(c == 1, lo, 0.0))
    if with_blocks:
        ex = jnp.where(kpos // SLC_BLOCK == c - EXTRA_BLOCK0, 1.0, ex)
    return ex.astype(BF16)


def _kv_prep_kernel(kc_src, vc_src, ks_src, vs_src, kw_src, vw_src, kgain_ref, pos_ref, w1_ref,
                    b1_ref, w2_ref, kc_ref, vct_ref, ks_ref, vst_ref, kw_ref, vwt_ref):
    seq = ks_src.shape[0]
    ks_ref[:, :HEAD_DIM] = _rms(ks_src[...], kgain_ref[1:2, :]).astype(BF16)
    ks_ref[:, HEAD_DIM:] = _key_extras(seq, True)
    kw_ref[:, :HEAD_DIM] = _rms(kw_src[...], kgain_ref[2:3, :]).astype(BF16)
    kw_ref[:, HEAD_DIM:] = _key_extras(seq, False)
    _store_values_t(vs_src, vst_ref)
    _store_values_t(vw_src, vwt_ref)
    kc = _compress(kc_src, pos_ref.at[0], w1_ref.at[0], b1_ref.at[0], w2_ref.at[0])
    kc_ref[...] = _rms(kc, kgain_ref[0:1, :]).astype(BF16)
    vc = _compress(vc_src, pos_ref.at[1], w1_ref.at[1], b1_ref.at[1], w2_ref.at[1])
    vct_ref[...] = vc.T.astype(BF16)


def _kv_prep(kv_srcs, batch, seq, k_gain, cmp_pos, w1, b1, w2):
    groups = seq // CMP_STRIDE
    tk = min(ATT_TQ, seq)
    src = lambda i: pl.BlockSpec((seq, HEAD_DIM), lambda b, h, c0=kv_srcs[i][1]: (b, c0 + h))
    full = lambda a: pl.BlockSpec(a.shape, lambda b, h: (0,) * a.ndim)
    spec4 = lambda r, c: pl.BlockSpec((None, None, r, c), lambda b, h: (b, h, 0, 0))
    shape4 = lambda r, c: jax.ShapeDtypeStruct((batch, B_KV_HEADS, r, c), BF16)
    vt_spec = pl.BlockSpec((None, None, seq // tk, VT_ROWS, tk), lambda b, h: (b, h, 0, 0, 0))
    vt_shape = jax.ShapeDtypeStruct((batch, B_KV_HEADS, seq // tk, VT_ROWS, tk), BF16)
    return pl.pallas_call(
        _kv_prep_kernel,
        grid=(batch, B_KV_HEADS),
        in_specs=[src(i) for i in range(6)] + [full(k_gain), full(cmp_pos), full(w1), full(b1), full(w2)],
        out_specs=[spec4(groups, HEAD_DIM), spec4(HEAD_DIM, groups), spec4(seq, KEY_COLS), vt_spec,
                   spec4(seq, KEY_COLS), vt_spec],
        out_shape=[shape4(groups, HEAD_DIM), shape4(HEAD_DIM, groups), shape4(seq, KEY_COLS), vt_shape,
                   shape4(seq, KEY_COLS), vt_shape],
        compiler_params=_params("parallel", "parallel"),
        name="kv_prep",
    )(*[a for a, _ in kv_srcs], k_gain, cmp_pos, w1, b1, w2)


def _nsa_kernel(slopes_ref, q_ref, bg_ref, gl_ref, kc_ref, vct_ref, ks_ref, vst_ref, kw_ref, vwt_ref,
                qgain_ref, ovt_ref, o_ref, qt_ref, m_ref, acc_ref, *, n_cmp):
    tq = q_ref.shape[0]
    n_grp = kc_ref.shape[0]
    n_slc = ovt_ref.shape[0]
    h = pl.program_id(1)
    i = pl.program_id(2)
    t0 = i * tq

    for g in range(B_GROUP):
        q = q_ref[:, g * HEAD_DIM:(g + 1) * HEAD_DIM]
        qt_ref[g, :HEAD_DIM, :] = (_rms(q, qgain_ref[...]) * (HEAD_DIM ** -0.5)).T.astype(BF16)

    n_idx = lax.broadcasted_iota(jnp.int32, (n_grp, tq), 0)
    t_idx = t0 + lax.broadcasted_iota(jnp.int32, (n_grp, tq), 1)
    dist_c = t_idx - (n_idx * CMP_STRIDE + (CMP_LEN - 1))
    valid_c = (n_idx < n_cmp) & (dist_c >= 0)
    dist_cf = dist_c.astype(F32)
    groups = range(B_GROUP)
    s_c = [_dot(kc_ref[...], qt_ref[g, :HEAD_DIM, :]) for g in groups]
    s_c = [jnp.where(valid_c, s - slopes_ref[h * B_GROUP + g] * dist_cf, NEG_INF) for g, s in zip(groups, s_c)]
    m_c = [jnp.max(s, axis=0, keepdims=True) for s in s_c]
    e_c = [jnp.where(valid_c, jnp.exp(s - m), 0.0) for s, m in zip(s_c, m_c)]
    p_c = [e / jnp.maximum(jnp.sum(e, axis=0, keepdims=True), 1e-30) for e in e_c]
    o_cmp = [_dot(vct_ref[...], p.astype(BF16)) for p in p_c]
    p_sum = sum(p_c[1:], p_c[0])

    imp = jnp.dot(ovt_ref[...], p_sum, precision=lax.Precision.HIGHEST, preferred_element_type=F32)
    j_idx = lax.broadcasted_iota(jnp.int32, (n_slc, tq), 0)
    j_f = j_idx.astype(F32)
    t_sel = t0 + lax.broadcasted_iota(jnp.int32, (n_slc, tq), 1)
    cur = t_sel // SLC_BLOCK
    forced = (j_idx == 0) | (j_idx == cur) | (j_idx == cur - 1)
    val = jnp.where(forced, FORCE_SCORE, jnp.where(j_idx * SLC_BLOCK <= t_sel, imp, -1.0))
    sel_t = jnp.zeros((n_slc, tq), F32)
    for _ in range(min(TOP_N, n_slc)):
        best = jnp.max(val, axis=0, keepdims=True)
        first = jnp.min(jnp.where(val == best, j_f, float(n_slc)), axis=0, keepdims=True)
        hit = j_f == first
        sel_t = jnp.where(hit, 1.0, sel_t)
        val = jnp.where(hit, -jnp.inf, val)

    unselected = (sel_t - 1.0) * (-NEG_INF)
    e_row = lax.broadcasted_iota(jnp.int32, (EXTRA_BLOCK0, tq), 0)
    pad = jnp.zeros((HEAD_DIM - EXTRA_BLOCK0 - n_slc, tq), F32)
    for g in range(B_GROUP):
        slope = slopes_ref[h * B_GROUP + g]
        alibi = jnp.where(e_row == 0, slope * ALIBI_SPLIT, jnp.where(e_row == 1, slope, 0.0))
        qt_ref[g, HEAD_DIM:, :] = jnp.concatenate([alibi, unselected, pad], axis=0).astype(BF16)

    key_r = lax.broadcasted_iota(jnp.int32, (tq, tq), 0)
    qry_c = lax.broadcasted_iota(jnp.int32, (tq, tq), 1)
    causal = key_r <= qry_c

    SLC, WIN = 0, 1
    branch_refs = {SLC: (ks_ref, vst_ref), WIN: (kw_ref, vwt_ref)}

    def tiles_update(tiles, first):
        heads = range(B_GROUP)
        s = []
        for br, kt, mask in tiles:
            k_blk = branch_refs[br][0][pl.ds(pl.multiple_of(kt * tq, tq), tq), :]
            s.append([_dot(k_blk, qt_ref[g]) for g in heads])
        s = [[sg if mask is None else jnp.where(mask, sg, NEG_INF) for sg in st]
             for st, (_, _, mask) in zip(s, tiles)]
        branches = sorted({br for br, _, _ in tiles})
        m_old, m_new = {}, {}
        for br in branches:
            for g in heads:
                m = None
                for st, (tb, _, _) in zip(s, tiles):
                    if tb == br:
                        tile_max = jnp.max(st[g], axis=0, keepdims=True)
                        m = tile_max if m is None else jnp.maximum(m, tile_max)
                if not first:
                    m_old[br, g] = m_ref[br * B_GROUP + g]
                    m = jnp.maximum(m_old[br, g], m)
                m_new[br, g] = m
        p = [[jnp.exp(st[g] - m_new[br, g]).astype(BF16) for g in heads] for st, (br, _, _) in zip(s, tiles)]
        pv = [[_dot(branch_refs[br][1][kt], pt[g]) for g in heads] for pt, (br, kt, _) in zip(p, tiles)]
        for br in branches:
            for g in heads:
                total = None
                for pvt, (tb, _, _) in zip(pv, tiles):
                    if tb == br:
                        total = pvt[g] if total is None else total + pvt[g]
                slot = br * B_GROUP + g
                if first:
                    acc_ref[slot] = total
                else:
                    acc_ref[slot] = jnp.exp(m_old[br, g] - m_new[br, g]) * acc_ref[slot] + total
                m_ref[slot] = m_new[br, g]

    def finish(br):
        return [acc_ref[br * B_GROUP + g, :HEAD_DIM, :] / acc_ref[br * B_GROUP + g, HEAD_DIM:HEAD_DIM + 1, :]
                for g in range(B_GROUP)]

    tiles_update([(SLC, i, causal), (WIN, i, causal)], True)

    def slc_group(j, carry):
        tiles_update([(SLC, SLC_GROUP * j + u, None) for u in range(SLC_GROUP)], False)
        return carry

    lax.fori_loop(0, i // SLC_GROUP, slc_group, 0)
    size = SLC_GROUP // 2
    while size >= 1:
        @pl.when((i % (2 * size)) >= size)
        def _(size=size):
            first_tile = i // (2 * size) * (2 * size)
            tiles_update([(SLC, first_tile + u, None) for u in range(size)], False)
        size //= 2

    win_tiles = (WINDOW + tq - 1) // tq

    def win_tile(off):
        needs_mask = off * tq + tq - 1 >= WINDOW
        return (WIN, i - off, ((qry_c - key_r) + off * tq < WINDOW) if needs_mask else None)

    for count in range(1, win_tiles + 1):
        @pl.when(jnp.minimum(i, win_tiles) == count)
        def _(count=count):
            tiles_update([win_tile(off) for off in range(1, count + 1)], False)

    o_slc = finish(SLC)
    o_win = finish(WIN)

    gates_t = jax.nn.sigmoid(gl_ref[...].T)
    for g in range(B_GROUP):
        c = slice(g * HEAD_DIM, (g + 1) * HEAD_DIM)
        o_t = (gates_t[g:g + 1] * o_cmp[g]
               + gates_t[B_GROUP + g:B_GROUP + g + 1] * o_slc[g]
               + gates_t[2 * B_GROUP + g:2 * B_GROUP + g + 1] * o_win[g])
        o_ref[:, c] = (o_t.T * jax.nn.silu(bg_ref[:, c])).astype(BF16)


def _overlap_t(n_grp, n_cmp, n_slc):
    cs = np.arange(n_grp)[None, :] * CMP_STRIDE
    ss = np.arange(n_slc)[:, None] * SLC_BLOCK
    ov = np.clip(np.minimum(cs + CMP_LEN, ss + SLC_BLOCK) - np.maximum(cs, ss), 0, None) / CMP_LEN
    ov = ov * (np.arange(n_grp)[None, :] < n_cmp)
    return jnp.asarray(ov, dtype=F32)


def _nsa_attn(z, zg, kc, vct, ks, vst, kw, vwt, q_gain, batch, seq, q_col0, bg_col0, zg_col0):
    t = z.shape[0]
    tq = min(ATT_TQ, seq)
    assert seq % tq == 0 and tq % SLC_BLOCK == 0 and seq // SLC_BLOCK <= LANES
    nq = seq // tq
    n_grp = seq // CMP_STRIDE
    n_cmp = (seq - CMP_LEN) // CMP_STRIDE + 1
    n_slc = seq // SLC_BLOCK
    gw = B_GROUP * HEAD_DIM
    h_idx = jnp.arange(1, B_HEADS + 1, dtype=F32)
    slopes = jnp.exp2(-8.0 * h_idx / B_HEADS)
    ovt = _overlap_t(n_grp, n_cmp, n_slc)
    rows = lambda b, h, i: b * nq + i
    spec4 = lambda r, c: pl.BlockSpec((None, None, r, c), lambda b, h, i: (b, h, 0, 0))
    vt_spec = pl.BlockSpec((None, None, nq, VT_ROWS, tq), lambda b, h, i: (b, h, 0, 0, 0))
    full = lambda a: pl.BlockSpec(a.shape, lambda b, h, i: (0,) * a.ndim)
    qg = q_gain.reshape(1, HEAD_DIM)
    return pl.pallas_call(
        functools.partial(_nsa_kernel, n_cmp=n_cmp),
        grid=(batch, B_KV_HEADS, nq),
        in_specs=[
            pl.BlockSpec(memory_space=pltpu.SMEM),
            pl.BlockSpec((tq, gw), lambda b, h, i: (rows(b, h, i), q_col0 + h)),
            pl.BlockSpec((tq, gw), lambda b, h, i: (rows(b, h, i), bg_col0 + h)),
            pl.BlockSpec((tq, LANES), lambda b, h, i: (rows(b, h, i), zg_col0 + h)),
            spec4(n_grp, HEAD_DIM), spec4(HEAD_DIM, n_grp), spec4(seq, KEY_COLS), vt_spec,
            spec4(seq, KEY_COLS), vt_spec, full(qg), full(ovt),
        ],
        out_specs=pl.BlockSpec((tq, gw), lambda b, h, i: (rows(b, h, i), h)),
        out_shape=jax.ShapeDtypeStruct((t, B_WIDTH), BF16),
        scratch_shapes=[
            pltpu.VMEM((B_GROUP, KEY_COLS, tq), BF16),
            pltpu.VMEM((2 * B_GROUP, 1, tq), F32),
            pltpu.VMEM((2 * B_GROUP, VT_ROWS, tq), F32),
        ],
        compiler_params=_params("parallel", "parallel", "arbitrary"),
        name="nsa_attn",
    )(slopes, z, z, zg, kc, vct, ks, vst, kw, vwt, qg, ovt)


def _hgrn_kernel(q_ref, f_ref, i_ref, gate_ref, lb_ref, og_ref, o_ref, state_ref, *, layer):
    n_rows = q_ref.shape[0]
    n_heads = q_ref.shape[1] // C_DIM
    mid = C_CHUNK // 2 - 1

    logits = lb_ref[...]
    e = jnp.exp(logits - jnp.max(logits, axis=0, keepdims=True))
    p = e / jnp.sum(e, axis=0, keepdims=True)
    lb = jnp.sum(p[:layer + 1], axis=0, keepdims=True) - p[0:1]

    r_i = lax.broadcasted_iota(jnp.int32, (C_CHUNK, C_CHUNK), 0)
    c_i = lax.broadcasted_iota(jnp.int32, (C_CHUNK, C_CHUNK), 1)
    lower = r_i >= c_i
    tri = jnp.where(lower, 1.0, 0.0).astype(BF16)

    @pl.when(pl.program_id(2) == 0)
    def _():
        state_ref[...] = jnp.zeros_like(state_ref)

    def body(c, carry):
        r0 = pl.multiple_of(c * C_CHUNK, C_CHUNK)
        rows = pl.ds(r0, C_CHUNK)
        f = lb + (1.0 - lb) * jax.nn.sigmoid(f_ref[rows, :])
        k = 1.0 - f
        g = jnp.log(f)
        g_hi = g.astype(BF16)
        g_rest = g - g_hi.astype(F32)
        g_mid = g_rest.astype(BF16)
        g_lo = (g_rest - g_mid.astype(F32)).astype(BF16)
        gcum = _dot(tri, g_hi) + _dot(tri, g_mid) + _dot(tri, g_lo)
        g_mid = gcum[mid:mid + 1, :]
        g_end = gcum[C_CHUNK - 1:C_CHUNK, :]
        qa = q_ref[rows, :] * jnp.exp(gcum - g_mid)
        kb = k * jnp.exp(g_mid - gcum)
        q_in = (qa * jnp.exp(g_mid)).astype(BF16)
        k_st = (kb * jnp.exp(g_end - g_mid)).astype(BF16)
        qa = qa.astype(BF16)
        kb = kb.astype(BF16)
        v = i_ref[rows, :].astype(BF16)
        decay = jnp.exp(g_end)
        heads = range(n_heads)
        cs = [slice(hh * C_DIM, (hh + 1) * C_DIM) for hh in heads]
        attn = [_dot_nt(qa[:, c], kb[:, c]) for c in cs]
        state = [state_ref[hh] for hh in heads]
        inter = [_dot(q_in[:, c], s.astype(BF16)) for c, s in zip(cs, state)]
        update = [_dot_tn(k_st[:, c], v[:, c]) for c in cs]
        attn = [jnp.where(lower, a, 0.0).astype(BF16) for a in attn]
        o = [_dot(a, v[:, c]) + x for a, c, x in zip(attn, cs, inter)]
        for hh in heads:
            decay_rows = jnp.broadcast_to(decay[:, cs[hh]], (C_DIM, C_DIM)).T
            state_ref[hh] = decay_rows * state[hh] + update[hh]
        for hh in heads:
            out = _rms(o[hh], og_ref[...]) * jax.nn.silu(gate_ref[rows, cs[hh]])
            o_ref[rows, cs[hh]] = out.astype(BF16)
        return carry

    lax.fori_loop(0, n_rows // C_CHUNK, body, 0, unroll=2)


def _hgrn(z, lb_logits, out_gain, batch, seq, layer):
    t = z.shape[0]
    width = C_HEADS * C_DIM
    bw = HGRN_HEADS * C_DIM
    nb = width // bw
    rows = min(HGRN_ROWS, seq)
    ns = seq // rows
    assert seq % rows == 0 and rows % C_CHUNK == 0
    depth = lb_logits.shape[0]
    col = lambda part: pl.BlockSpec((rows, bw), lambda b, j, s, part=part: (b * ns + s, part * nb + j))
    og = out_gain.reshape(1, C_DIM)
    return pl.pallas_call(
        functools.partial(_hgrn_kernel, layer=layer),
        grid=(batch, nb, ns),
        in_specs=[col(0), col(1), col(2), col(3),
                  pl.BlockSpec((depth, bw), lambda b, j, s: (0, j)),
                  pl.BlockSpec((1, C_DIM), lambda b, j, s: (0, 0))],
        out_specs=pl.BlockSpec((rows, bw), lambda b, j, s: (b * ns + s, j)),
        out_shape=jax.ShapeDtypeStruct((t, width), BF16),
        scratch_shapes=[pltpu.VMEM((HGRN_HEADS, C_DIM, C_DIM), F32)],
        compiler_params=_params("parallel", "parallel", "arbitrary"),
        name="hgrn",
    )(z, z, z, z, lb_logits, og)


def _even_layer(h, batch, seq, norm_g, w_in, ln_g, ln_b, sgu_w, sgu_b, q_gain, k_gain,
                cmp_pos, cmp_w1, cmp_b1, cmp_w2, w_out, next_gain):
    d = h.shape[1]
    n_main = 3 * A_WIDTH + 2 * B_WIDTH + 6 * B_KV_HEADS * HEAD_DIM
    wg = w_in[:, n_main:].reshape(d, 3, B_KV_HEADS, B_GROUP).transpose(0, 2, 1, 3)
    wg = wg.reshape(d, B_KV_HEADS, 3 * B_GROUP)
    wg = jnp.pad(wg, ((0, 0), (0, 0), (0, LANES - 3 * B_GROUP))).reshape(d, B_KV_HEADS * LANES)
    w_all = jnp.concatenate([w_in[:, :n_main].astype(BF16), wg.astype(BF16)], axis=1)
    z = _matmul(_rms_norm(h, norm_g), w_all)

    sgu_b_full = jnp.broadcast_to(sgu_b[:, :, None], (A_GROUPS, A_CHUNK, HEAD_DIM))
    a_out = _mixer_a(z, ln_g, ln_b, sgu_w, sgu_b_full)

    gw = B_GROUP * HEAD_DIM
    q_col0 = 3 * A_WIDTH // gw
    bg_col0 = (3 * A_WIDTH + B_WIDTH) // gw
    kv_col0 = (3 * A_WIDTH + 2 * B_WIDTH) // HEAD_DIM
    kv_srcs = [(z, kv_col0 + i * B_KV_HEADS) for i in range(6)]
    zg_col0 = n_main // LANES
    w1 = cmp_w1.reshape(2, CMP_LEN, HEAD_DIM, CMP_HIDDEN).astype(BF16)
    kc, vct, ks, vst, kw, vwt = _kv_prep(kv_srcs, batch, seq, k_gain, cmp_pos, w1,
                                         cmp_b1.reshape(2, 1, CMP_HIDDEN), cmp_w2.astype(BF16))
    b_out = _nsa_attn(z, z, kc, vct, ks, vst, kw, vwt, q_gain, batch, seq, q_col0, bg_col0, zg_col0)

    w_out = w_out.astype(BF16)
    return _matmul_residual(h, [a_out, b_out], [w_out[:A_WIDTH], w_out[A_WIDTH:]], next_gain)


def _odd_layer(h, hn, batch, seq, layer, norm_g, w_in, lb_logits, out_gain, w_out, next_gain):
    w_in = w_in.astype(BF16)
    z = _matmul(hn if hn is not None else _rms_norm(h, norm_g), w_in)
    o = _hgrn(z, lb_logits, out_gain, batch, seq, layer)
    return _matmul_residual(h, [o], [w_out.astype(BF16)], next_gain)


def kernel(x, even_norm, even_w_in, sgu_ln_g, sgu_ln_b, sgu_w, sgu_b, nsa_q_gain, nsa_k_gain, cmp_pos, cmp_w1, cmp_b1, cmp_w2, even_w_out, odd_norm, odd_w_in, hgrn_lb, hgrn_out_gain, odd_w_out):
    batch, seq, d = x.shape
    depth = hgrn_lb.shape[0]
    h = x.reshape(batch * seq, d)
    hn = None
    for layer in range(depth):
        next_gain = odd_norm[(layer + 1) // 2] if (layer + 1 < depth and layer % 2 == 0) else None
        if layer % 2 == 0:
            e = layer // 2
            h, hn = _even_layer(h, batch, seq, even_norm[e], even_w_in[e], sgu_ln_g[e], sgu_ln_b[e],
                                sgu_w[e], sgu_b[e], nsa_q_gain[e], nsa_k_gain[e], cmp_pos[e], cmp_w1[e],
                                cmp_b1[e], cmp_w2[e], even_w_out[e], next_gain)
        else:
            o = layer // 2
            h, hn = _odd_layer(h, hn, batch, seq, layer, odd_norm[o], odd_w_in[o], hgrn_lb,
                               hgrn_out_gain[o], odd_w_out[o], next_gain)
    return h.reshape(batch, seq, d)
```

```python
import functools

import numpy as np
import jax
import jax.numpy as jnp
from jax import lax
from jax.experimental import pallas as pl
from jax.experimental.pallas import tpu as pltpu

HEAD_DIM = 128
A_GROUPS = 8
A_WIDTH = A_GROUPS * HEAD_DIM
A_CHUNK = 128
B_HEADS = 8
B_KV_HEADS = 2
B_GROUP = B_HEADS // B_KV_HEADS
B_WIDTH = B_HEADS * HEAD_DIM
CMP_LEN = 32
CMP_STRIDE = 16
CMP_HIDDEN = 256
SLC_BLOCK = 64
TOP_N = 8
WINDOW = 512
C_HEADS = 16
C_DIM = 128
C_CHUNK = 64
EPS = 1e-6
NEG_INF = -1e30
FORCE_SCORE = 1e6

LANES = 128
BF16_SUBLANES = 16
VMEM_LIMIT_BYTES = 56 * 1024 * 1024

PROJ_TM = 1024
PLAIN_TN = 2304
RESID_TM = 512
RESID_TN = 2048
NORM_ROWS = 512
MIXA_ROWS = 512
ATT_TQ = 256
SLC_GROUP = 4
VT_ROWS = HEAD_DIM + BF16_SUBLANES
KEY_COLS = 2 * HEAD_DIM
ALIBI_SPLIT = 64
EXTRA_BLOCK0 = BF16_SUBLANES
HGRN_HEADS = 8
HGRN_ROWS = 1024

BF16 = jnp.bfloat16
F32 = jnp.float32


def _params(*sem):
    return pltpu.CompilerParams(dimension_semantics=sem, vmem_limit_bytes=VMEM_LIMIT_BYTES)


def _col_tile(n, limit):
    if n <= limit:
        return n
    return max(c for c in range(LANES, limit + 1, LANES) if n % c == 0)


def _dot(a, b):
    return jnp.dot(a, b, preferred_element_type=F32)


def _dot_nt(a, b):
    return lax.dot_general(a, b, (((1,), (1,)), ((), ())), preferred_element_type=F32)


def _dot_tn(a, b):
    return lax.dot_general(a, b, (((0,), (0,)), ((), ())), preferred_element_type=F32)


def _rms(x, gain):
    return x * lax.rsqrt(jnp.mean(x * x, axis=-1, keepdims=True) + EPS) * gain


def _rms_norm_kernel(x_ref, g_ref, o_ref):
    o_ref[...] = _rms(x_ref[...], g_ref[...]).astype(BF16)


def _rms_norm(x, gain):
    t, d = x.shape
    tm = min(NORM_ROWS, t)
    assert t % tm == 0
    return pl.pallas_call(
        _rms_norm_kernel,
        grid=(t // tm,),
        in_specs=[pl.BlockSpec((tm, d), lambda i: (i, 0)), pl.BlockSpec((1, d), lambda i: (0, 0))],
        out_specs=pl.BlockSpec((tm, d), lambda i: (i, 0)),
        out_shape=jax.ShapeDtypeStruct((t, d), BF16),
        compiler_params=_params("parallel"),
        name="rms_norm",
    )(x, gain.reshape(1, d))


def _matmul_residual_kernel(x_ref, *refs, n_pairs, emit_norm):
    acts, ws, rest = refs[:n_pairs], refs[n_pairs:2 * n_pairs], refs[2 * n_pairs:]
    acc = x_ref[...]
    for a_ref, w_ref in zip(acts, ws):
        acc = acc + _dot(a_ref[...], w_ref[...])
    if emit_norm:
        gain_ref, o_ref, xn_ref = rest
        xn_ref[...] = _rms(acc, gain_ref[...]).astype(BF16)
    else:
        (o_ref,) = rest
    o_ref[...] = acc


def _matmul_residual(x, acts, ws, next_gain=None):
    t, d = x.shape
    tm = min(RESID_TM, t)
    tn = _col_tile(d, RESID_TN)
    assert t % tm == 0 and d % tn == 0
    emit_norm = next_gain is not None and tn == d
    in_specs = [pl.BlockSpec((tm, tn), lambda i, j: (i, j))]
    in_specs += [pl.BlockSpec((tm, a.shape[1]), lambda i, j: (i, 0)) for a in acts]
    in_specs += [pl.BlockSpec((w.shape[0], tn), lambda i, j: (0, j)) for w in ws]
    out_specs = [pl.BlockSpec((tm, tn), lambda i, j: (i, j))]
    out_shape = [jax.ShapeDtypeStruct((t, d), F32)]
    args = [x, *acts, *ws]
    if emit_norm:
        in_specs.append(pl.BlockSpec((1, d), lambda i, j: (0, 0)))
        out_specs.append(pl.BlockSpec((tm, d), lambda i, j: (i, 0)))
        out_shape.append(jax.ShapeDtypeStruct((t, d), BF16))
        args.append(next_gain.reshape(1, d))
    out = pl.pallas_call(
        functools.partial(_matmul_residual_kernel, n_pairs=len(acts), emit_norm=emit_norm),
        grid=(t // tm, d // tn),
        in_specs=in_specs,
        out_specs=out_specs,
        out_shape=out_shape,
        compiler_params=_params("parallel", "arbitrary"),
        name="matmul_residual",
    )(*args)
    return (out[0], out[1]) if emit_norm else (out[0], None)


def _matmul_kernel(x_ref, w_ref, o_ref):
    o_ref[...] = _dot(x_ref[...], w_ref[...])


def _matmul(xn, w):
    t, d = xn.shape
    n = w.shape[1]
    tm = min(PROJ_TM, t)
    tn = _col_tile(n, PLAIN_TN)
    assert t % tm == 0 and n % tn == 0
    return pl.pallas_call(
        _matmul_kernel,
        grid=(t // tm, n // tn),
        in_specs=[pl.BlockSpec((tm, d), lambda i, j: (i, 0)), pl.BlockSpec((d, tn), lambda i, j: (0, j))],
        out_specs=pl.BlockSpec((tm, tn), lambda i, j: (i, j)),
        out_shape=jax.ShapeDtypeStruct((t, n), F32),
        compiler_params=_params("parallel", "arbitrary"),
        name="matmul",
    )(xn, w)


def _mixer_a_kernel(u_ref, v_ref, gate_ref, lng_ref, lnb_ref, w_ref, b_ref, o_ref):
    rows = u_ref.shape[0]
    v = jax.nn.gelu(v_ref[...])
    mu = jnp.mean(v, axis=-1, keepdims=True)
    vc = v - mu
    var = jnp.mean(vc * vc, axis=-1, keepdims=True)
    vn = (vc * lax.rsqrt(var + EPS) * lng_ref[...] + lnb_ref[...]).astype(BF16)
    tr = lax.broadcasted_iota(jnp.int32, (A_CHUNK, A_CHUNK), 0)
    tc = lax.broadcasted_iota(jnp.int32, (A_CHUNK, A_CHUNK), 1)
    for g in range(A_GROUPS):
        cols = slice(g * HEAD_DIM, (g + 1) * HEAD_DIM)
        wm = jnp.where(tr >= tc, w_ref[g], 0.0).astype(BF16)
        for c in range(rows // A_CHUNK):
            rs = slice(c * A_CHUNK, (c + 1) * A_CHUNK)
            sv = _dot(wm, vn[rs, cols]) + b_ref[g]
            u = jax.nn.gelu(u_ref[rs, cols])
            o_ref[rs, cols] = (u * sv * jax.nn.silu(gate_ref[rs, cols])).astype(BF16)


def _mixer_a(z, ln_g, ln_b, sgu_w, sgu_b_full):
    t = z.shape[0]
    rows = min(MIXA_ROWS, t)
    assert t % rows == 0 and rows % A_CHUNK == 0
    blk = lambda c: pl.BlockSpec((rows, A_WIDTH), lambda i: (i, c))
    full = lambda a: pl.BlockSpec(a.shape, lambda i: (0,) * a.ndim)
    lng = ln_g.reshape(1, A_WIDTH)
    lnb = ln_b.reshape(1, A_WIDTH)
    return pl.pallas_call(
        _mixer_a_kernel,
        grid=(t // rows,),
        in_specs=[blk(0), blk(1), blk(2), full(lng), full(lnb), full(sgu_w), full(sgu_b_full)],
        out_specs=pl.BlockSpec((rows, A_WIDTH), lambda i: (i, 0)),
        out_shape=jax.ShapeDtypeStruct((t, A_WIDTH), BF16),
        compiler_params=_params("parallel"),
        name="mixer_a",
    )(z, z, z, lng, lnb, sgu_w, sgu_b_full)


def _compress(src_ref, pos_ref, w1_ref, b1_ref, w2_ref):
    groups = src_ref.shape[0] // CMP_STRIDE
    lo = jnp.zeros((groups, CMP_HIDDEN), F32)
    hi = jnp.zeros((groups, CMP_HIDDEN), F32)
    for l in range(CMP_STRIDE):
        xl = src_ref[pl.ds(l, groups, stride=CMP_STRIDE), :]
        lo = lo + _dot((xl + pos_ref[l:l + 1, :]).astype(BF16), w1_ref[l])
        hi = hi + _dot((xl + pos_ref[CMP_STRIDE + l:CMP_STRIDE + l + 1, :]).astype(BF16),
                       w1_ref[CMP_STRIDE + l])
    hid = jax.nn.gelu(lo + pltpu.roll(hi, groups - 1, 0) + b1_ref[...])
    return _dot(hid.astype(BF16), w2_ref[...])


def _store_values_t(v_src, vt_ref):
    n_tiles, rows, tk = vt_ref.shape
    for kt in range(n_tiles):
        vt_ref[kt, :HEAD_DIM, :] = v_src[kt * tk:(kt + 1) * tk, :].T.astype(BF16)
        vt_ref[kt, HEAD_DIM:, :] = jnp.ones((rows - HEAD_DIM, tk), BF16)


def _key_extras(seq, with_blocks):
    kpos = lax.broadcasted_iota(jnp.int32, (seq, LANES), 0)
    c = lax.broadcasted_iota(jnp.int32, (seq, LANES), 1)
    hi = (kpos // ALIBI_SPLIT).astype(F32)
    lo = (kpos % ALIBI_SPLIT).astype(F32)
    ex = jnp.where(c == 0, hi, jnp.where(c == 1, lo, 0.0))
    if with_blocks:
        ex = jnp.where(kpos // SLC_BLOCK == c - EXTRA_BLOCK0, 1.0, ex)
    return ex.astype(BF16)


def _kv_prep_kernel(kc_src, vc_src, ks_src, vs_src, kw_src, vw_src, kgain_ref, pos_ref, w1_ref,
                    b1_ref, w2_ref, kc_ref, vct_ref, ks_ref, vst_ref, kw_ref, vwt_ref):
    seq = ks_src.shape[0]
    ks_ref[:, :HEAD_DIM] = _rms(ks_src[...], kgain_ref[1:2, :]).astype(BF16)
    ks_ref[:, HEAD_DIM:] = _key_extras(seq, True)
    kw_ref[:, :HEAD_DIM] = _rms(kw_src[...], kgain_ref[2:3, :]).astype(BF16)
    kw_ref[:, HEAD_DIM:] = _key_extras(seq, False)
    _store_values_t(vs_src, vst_ref)
    _store_values_t(vw_src, vwt_ref)
    kc = _compress(kc_src, pos_ref.at[0], w1_ref.at[0], b1_ref.at[0], w2_ref.at[0])
    kc_ref[...] = _rms(kc, kgain_ref[0:1, :]).astype(BF16)
    vc = _compress(vc_src, pos_ref.at[1], w1_ref.at[1], b1_ref.at[1], w2_ref.at[1])
    vct_ref[...] = vc.T.astype(BF16)


def _kv_prep(kv_srcs, batch, seq, k_gain, cmp_pos, w1, b1, w2):
    groups = seq // CMP_STRIDE
    tk = min(ATT_TQ, seq)
    src = lambda i: pl.BlockSpec((seq, HEAD_DIM), lambda b, h, c0=kv_srcs[i][1]: (b, c0 + h))
    full = lambda a: pl.BlockSpec(a.shape, lambda b, h: (0,) * a.ndim)
    spec4 = lambda r, c: pl.BlockSpec((None, None, r, c), lambda b, h: (b, h, 0, 0))
    shape4 = lambda r, c: jax.ShapeDtypeStruct((batch, B_KV_HEADS, r, c), BF16)
    vt_spec = pl.BlockSpec((None, None, seq // tk, VT_ROWS, tk), lambda b, h: (b, h, 0, 0, 0))
    vt_shape = jax.ShapeDtypeStruct((batch, B_KV_HEADS, seq // tk, VT_ROWS, tk), BF16)
    return pl.pallas_call(
        _kv_prep_kernel,
        grid=(batch, B_KV_HEADS),
        in_specs=[src(i) for i in range(6)] + [full(k_gain), full(cmp_pos), full(w1), full(b1), full(w2)],
        out_specs=[spec4(groups, HEAD_DIM), spec4(HEAD_DIM, groups), spec4(seq, KEY_COLS), vt_spec,
                   spec4(seq, KEY_COLS), vt_spec],
        out_shape=[shape4(groups, HEAD_DIM), shape4(HEAD_DIM, groups), shape4(seq, KEY_COLS), vt_shape,
                   shape4(seq, KEY_COLS), vt_shape],
        compiler_params=_params("parallel", "parallel"),
        name="kv_prep",
    )(*[a for a, _ in kv_srcs], k_gain, cmp_pos, w1, b1, w2)


def _nsa_kernel(slopes_ref, q_ref, bg_ref, gl_ref, kc_ref, vct_ref, ks_ref, vst_ref, kw_ref, vwt_ref,
                qgain_ref, ovt_ref, o_ref, qt_ref, m_ref, acc_ref, *, n_cmp):
    tq = q_ref.shape[0]
    n_grp = kc_ref.shape[0]
    n_slc = ovt_ref.shape[0]
    h = pl.program_id(1)
    i = pl.program_id(2)
    t0 = i * tq

    for g in range(B_GROUP):
        q = q_ref[:, g * HEAD_DIM:(g + 1) * HEAD_DIM]
        qt_ref[g, :HEAD_DIM, :] = (_rms(q, qgain_ref[...]) * (HEAD_DIM ** -0.5)).T.astype(BF16)

    n_idx = lax.broadcasted_iota(jnp.int32, (n_grp, tq), 0)
    t_idx = t0 + lax.broadcasted_iota(jnp.int32, (n_grp, tq), 1)
    dist_c = t_idx - (n_idx * CMP_STRIDE + (CMP_LEN - 1))
    valid_c = (n_idx < n_cmp) & (dist_c >= 0)
    dist_cf = dist_c.astype(F32)
    groups = range(B_GROUP)
    s_c = [_dot(kc_ref[...], qt_ref[g, :HEAD_DIM, :]) for g in groups]
    s_c = [jnp.where(valid_c, s - slopes_ref[h * B_GROUP + g] * dist_cf, NEG_INF) for g, s in zip(groups, s_c)]
    m_c = [jnp.max(s, axis=0, keepdims=True) for s in s_c]
    e_c = [jnp.where(valid_c, jnp.exp(s - m), 0.0) for s, m in zip(s_c, m_c)]
    p_c = [e / jnp.maximum(jnp.sum(e, axis=0, keepdims=True), 1e-30) for e in e_c]
    o_cmp = [_dot(vct_ref[...], p.astype(BF16)) for p in p_c]
    p_sum = sum(p_c[1:], p_c[0])

    imp = jnp.dot(ovt_ref[...], p_sum, precision=lax.Precision.HIGHEST, preferred_element_type=F32)
    j_idx = lax.broadcasted_iota(jnp.int32, (n_slc, tq), 0)
    j_f = j_idx.astype(F32)
    t_sel = t0 + lax.broadcasted_iota(jnp.int32, (n_slc, tq), 1)
    cur = t_sel // SLC_BLOCK
    forced = (j_idx == 0) | (j_idx == cur) | (j_idx == cur - 1)
    val = jnp.where(forced, FORCE_SCORE, jnp.where(j_idx * SLC_BLOCK <= t_sel, imp, -1.0))
    sel_t = jnp.zeros((n_slc, tq), F32)
    for _ in range(min(TOP_N, n_slc)):
        best = jnp.max(val, axis=0, keepdims=True)
        first = jnp.min(jnp.where(val == best, j_f, float(n_slc)), axis=0, keepdims=True)
        hit = j_f == first
        sel_t = jnp.where(hit, 1.0, sel_t)
        val = jnp.where(hit, -jnp.inf, val)

    unselected = (sel_t - 1.0) * (-NEG_INF)
    e_row = lax.broadcasted_iota(jnp.int32, (EXTRA_BLOCK0, tq), 0)
    pad = jnp.zeros((HEAD_DIM - EXTRA_BLOCK0 - n_slc, tq), F32)
    for g in range(B_GROUP):
        slope = slopes_ref[h * B_GROUP + g]
        alibi = jnp.where(e_row == 0, slope * ALIBI_SPLIT, jnp.where(e_row == 1, slope, 0.0))
        qt_ref[g, HEAD_DIM:, :] = jnp.concatenate([alibi, unselected, pad], axis=0).astype(BF16)

    key_r = lax.broadcasted_iota(jnp.int32, (tq, tq), 0)
    qry_c = lax.broadcasted_iota(jnp.int32, (tq, tq), 1)
    causal = key_r <= qry_c

    SLC, WIN = 0, 1
    branch_refs = {SLC: (ks_ref, vst_ref), WIN: (kw_ref, vwt_ref)}

    def tiles_update(tiles, first):
        heads = range(B_GROUP)
        s = []
        for br, kt, mask in tiles:
            k_blk = branch_refs[br][0][pl.ds(pl.multiple_of(kt * tq, tq), tq), :]
            s.append([_dot(k_blk, qt_ref[g]) for g in heads])
        s = [[sg if mask is None else jnp.where(mask, sg, NEG_INF) for sg in st]
             for st, (_, _, mask) in zip(s, tiles)]
        branches = sorted({br for br, _, _ in tiles})
        m_old, m_new = {}, {}
        for br in branches:
            for g in heads:
                m = None
                for st, (tb, _, _) in zip(s, tiles):
                    if tb == br:
                        tile_max = jnp.max(st[g], axis=0, keepdims=True)
                        m = tile_max if m is None else jnp.maximum(m, tile_max)
                if not first:
                    m_old[br, g] = m_ref[br * B_GROUP + g]
                    m = jnp.maximum(m_old[br, g], m)
                m_new[br, g] = m
        p = [[jnp.exp(st[g] - m_new[br, g]).astype(BF16) for g in heads] for st, (br, _, _) in zip(s, tiles)]
        pv = [[_dot(branch_refs[br][1][kt], pt[g]) for g in heads] for pt, (br, kt, _) in zip(p, tiles)]
        for br in branches:
            for g in heads:
                total = None
                for pvt, (tb, _, _) in zip(pv, tiles):
                    if tb == br:
                        total = pvt[g] if total is None else total + pvt[g]
                slot = br * B_GROUP + g
                if first:
                    acc_ref[slot] = total
                else:
                    acc_ref[slot] = jnp.exp(m_old[br, g] - m_new[br, g]) * acc_ref[slot] + total
                m_ref[slot] = m_new[br, g]

    def finish(br):
        return [acc_ref[br * B_GROUP + g, :HEAD_DIM, :] / acc_ref[br * B_GROUP + g, HEAD_DIM:HEAD_DIM + 1, :]
                for g in range(B_GROUP)]

    win_tiles = (WINDOW + tq - 1) // tq

    def win_tile(off):
        needs_mask = off * tq + tq - 1 >= WINDOW
        return (WIN, i - off, ((qry_c - key_r) + off * tq < WINDOW) if needs_mask else None)

    for count in range(win_tiles + 1):
        @pl.when(jnp.minimum(i, win_tiles) == count)
        def _(count=count):
            tiles_update([(SLC, i, causal), (WIN, i, causal)] + [win_tile(off) for off in range(1, count + 1)],
                         True)

    def slc_group(j, carry):
        tiles_update([(SLC, SLC_GROUP * j + u, None) for u in range(SLC_GROUP)], False)
        return carry

    lax.fori_loop(0, i // SLC_GROUP, slc_group, 0)
    size = SLC_GROUP // 2
    while size >= 1:
        @pl.when((i % (2 * size)) >= size)
        def _(size=size):
            first_tile = i // (2 * size) * (2 * size)
            tiles_update([(SLC, first_tile + u, None) for u in range(size)], False)
        size //= 2

    o_slc = finish(SLC)
    o_win = finish(WIN)

    gates_t = jax.nn.sigmoid(gl_ref[...].T)
    for g in range(B_GROUP):
        c = slice(g * HEAD_DIM, (g + 1) * HEAD_DIM)
        o_t = (gates_t[g:g + 1] * o_cmp[g]
               + gates_t[B_GROUP + g:B_GROUP + g + 1] * o_slc[g]
               + gates_t[2 * B_GROUP + g:2 * B_GROUP + g + 1] * o_win[g])
        o_ref[:, c] = (o_t.T * jax.nn.silu(bg_ref[:, c])).astype(BF16)


def _overlap_t(n_grp, n_cmp, n_slc):
    cs = np.arange(n_grp)[None, :] * CMP_STRIDE
    ss = np.arange(n_slc)[:, None] * SLC_BLOCK
    ov = np.clip(np.minimum(cs + CMP_LEN, ss + SLC_BLOCK) - np.maximum(cs, ss), 0, None) / CMP_LEN
    ov = ov * (np.arange(n_grp)[None, :] < n_cmp)
    return jnp.asarray(ov, dtype=F32)


def _nsa_attn(z, zg, kc, vct, ks, vst, kw, vwt, q_gain, batch, seq, q_col0, bg_col0, zg_col0):
    t = z.shape[0]
    tq = min(ATT_TQ, seq)
    assert seq % tq == 0 and tq % SLC_BLOCK == 0 and seq // SLC_BLOCK <= LANES
    nq = seq // tq
    n_grp = seq // CMP_STRIDE
    n_cmp = (seq - CMP_LEN) // CMP_STRIDE + 1
    n_slc = seq // SLC_BLOCK
    gw = B_GROUP * HEAD_DIM
    h_idx = jnp.arange(1, B_HEADS + 1, dtype=F32)
    slopes = jnp.exp2(-8.0 * h_idx / B_HEADS)
    ovt = _overlap_t(n_grp, n_cmp, n_slc)
    rows = lambda b, h, i: b * nq + i
    spec4 = lambda r, c: pl.BlockSpec((None, None, r, c), lambda b, h, i: (b, h, 0, 0))
    vt_spec = pl.BlockSpec((None, None, nq, VT_ROWS, tq), lambda b, h, i: (b, h, 0, 0, 0))
    full = lambda a: pl.BlockSpec(a.shape, lambda b, h, i: (0,) * a.ndim)
    qg = q_gain.reshape(1, HEAD_DIM)
    return pl.pallas_call(
        functools.partial(_nsa_kernel, n_cmp=n_cmp),
        grid=(batch, B_KV_HEADS, nq),
        in_specs=[
            pl.BlockSpec(memory_space=pltpu.SMEM),
            pl.BlockSpec((tq, gw), lambda b, h, i: (rows(b, h, i), q_col0 + h)),
            pl.BlockSpec((tq, gw), lambda b, h, i: (rows(b, h, i), bg_col0 + h)),
            pl.BlockSpec((tq, LANES), lambda b, h, i: (rows(b, h, i), zg_col0 + h)),
            spec4(n_grp, HEAD_DIM), spec4(HEAD_DIM, n_grp), spec4(seq, KEY_COLS), vt_spec,
            spec4(seq, KEY_COLS), vt_spec, full(qg), full(ovt),
        ],
        out_specs=pl.BlockSpec((tq, gw), lambda b, h, i: (rows(b, h, i), h)),
        out_shape=jax.ShapeDtypeStruct((t, B_WIDTH), BF16),
        scratch_shapes=[
            pltpu.VMEM((B_GROUP, KEY_COLS, tq), BF16),
            pltpu.VMEM((2 * B_GROUP, 1, tq), F32),
            pltpu.VMEM((2 * B_GROUP, VT_ROWS, tq), F32),
        ],
        compiler_params=_params("parallel", "parallel", "arbitrary"),
        name="nsa_attn",
    )(slopes, z, z, zg, kc, vct, ks, vst, kw, vwt, qg, ovt)


def _hgrn_kernel(q_ref, f_ref, i_ref, gate_ref, lb_ref, og_ref, o_ref, state_ref, *, layer):
    n_rows = q_ref.shape[0]
    n_heads = q_ref.shape[1] // C_DIM
    mid = C_CHUNK // 2 - 1

    logits = lb_ref[...]
    e = jnp.exp(logits - jnp.max(logits, axis=0, keepdims=True))
    p = e / jnp.sum(e, axis=0, keepdims=True)
    lb = jnp.sum(p[:layer + 1], axis=0, keepdims=True) - p[0:1]

    r_i = lax.broadcasted_iota(jnp.int32, (C_CHUNK, C_CHUNK), 0)
    c_i = lax.broadcasted_iota(jnp.int32, (C_CHUNK, C_CHUNK), 1)
    lower = r_i >= c_i
    tri = jnp.where(lower, 1.0, 0.0).astype(BF16)

    @pl.when(pl.program_id(2) == 0)
    def _():
        state_ref[...] = jnp.zeros_like(state_ref)

    def body(c, carry):
        r0 = pl.multiple_of(c * C_CHUNK, C_CHUNK)
        rows = pl.ds(r0, C_CHUNK)
        f = lb + (1.0 - lb) * jax.nn.sigmoid(f_ref[rows, :])
        k = 1.0 - f
        g = jnp.log(f)
        g_hi = g.astype(BF16)
        g_rest = g - g_hi.astype(F32)
        g_mid = g_rest.astype(BF16)
        g_lo = (g_rest - g_mid.astype(F32)).astype(BF16)
        gcum = _dot(tri, g_hi) + _dot(tri, g_mid) + _dot(tri, g_lo)
        g_mid = gcum[mid:mid + 1, :]
        g_end = gcum[C_CHUNK - 1:C_CHUNK, :]
        qa = q_ref[rows, :] * jnp.exp(gcum - g_mid)
        kb = k * jnp.exp(g_mid - gcum)
        q_in = (qa * jnp.exp(g_mid)).astype(BF16)
        k_st = (kb * jnp.exp(g_end - g_mid)).astype(BF16)
        qa = qa.astype(BF16)
        kb = kb.astype(BF16)
        v = i_ref[rows, :].astype(BF16)
        decay = jnp.exp(g_end)
        heads = range(n_heads)
        cs = [slice(hh * C_DIM, (hh + 1) * C_DIM) for hh in heads]
        attn = [_dot_nt(qa[:, c], kb[:, c]) for c in cs]
        state = [state_ref[hh] for hh in heads]
        inter = [_dot(q_in[:, c], s.astype(BF16)) for c, s in zip(cs, state)]
        update = [_dot_tn(k_st[:, c], v[:, c]) for c in cs]
        attn = [jnp.where(lower, a, 0.0).astype(BF16) for a in attn]
        o = [_dot(a, v[:, c]) + x for a, c, x in zip(attn, cs, inter)]
        for hh in heads:
            decay_rows = jnp.broadcast_to(decay[:, cs[hh]], (C_DIM, C_DIM)).T
            state_ref[hh] = decay_rows * state[hh] + update[hh]
        for hh in heads:
            out = _rms(o[hh], og_ref[...]) * jax.nn.silu(gate_ref[rows, cs[hh]])
            o_ref[rows, cs[hh]] = out.astype(BF16)
        return carry

    lax.fori_loop(0, n_rows // C_CHUNK, body, 0, unroll=4)


def _hgrn(z, lb_logits, out_gain, batch, seq, layer):
    t = z.shape[0]
    width = C_HEADS * C_DIM
    bw = HGRN_HEADS * C_DIM
    nb = width // bw
    rows = min(HGRN_ROWS, seq)
    ns = seq // rows
    assert seq % rows == 0 and rows % C_CHUNK == 0
    depth = lb_logits.shape[0]
    col = lambda part: pl.BlockSpec((rows, bw), lambda b, j, s, part=part: (b * ns + s, part * nb + j))
    og = out_gain.reshape(1, C_DIM)
    return pl.pallas_call(
        functools.partial(_hgrn_kernel, layer=layer),
        grid=(batch, nb, ns),
        in_specs=[col(0), col(1), col(2), col(3),
                  pl.BlockSpec((depth, bw), lambda b, j, s: (0, j)),
                  pl.BlockSpec((1, C_DIM), lambda b, j, s: (0, 0))],
        out_specs=pl.BlockSpec((rows, bw), lambda b, j, s: (b * ns + s, j)),
        out_shape=jax.ShapeDtypeStruct((t, width), BF16),
        scratch_shapes=[pltpu.VMEM((HGRN_HEADS, C_DIM, C_DIM), F32)],
        compiler_params=_params("parallel", "parallel", "arbitrary"),
        name="hgrn",
    )(z, z, z, z, lb_logits, og)


def _even_layer(h, batch, seq, norm_g, w_in, ln_g, ln_b, sgu_w, sgu_b, q_gain, k_gain,
                cmp_pos, cmp_w1, cmp_b1, cmp_w2, w_out, next_gain):
    d = h.shape[1]
    n_main = 3 * A_WIDTH + 2 * B_WIDTH + 6 * B_KV_HEADS * HEAD_DIM
    wg = w_in[:, n_main:].reshape(d, 3, B_KV_HEADS, B_GROUP).transpose(0, 2, 1, 3)
    wg = wg.reshape(d, B_KV_HEADS, 3 * B_GROUP)
    wg = jnp.pad(wg, ((0, 0), (0, 0), (0, LANES - 3 * B_GROUP))).reshape(d, B_KV_HEADS * LANES)
    w_all = jnp.concatenate([w_in[:, :n_main].astype(BF16), wg.astype(BF16)], axis=1)
    z = _matmul(_rms_norm(h, norm_g), w_all)

    sgu_b_full = jnp.broadcast_to(sgu_b[:, :, None], (A_GROUPS, A_CHUNK, HEAD_DIM))
    a_out = _mixer_a(z, ln_g, ln_b, sgu_w, sgu_b_full)

    gw = B_GROUP * HEAD_DIM
    q_col0 = 3 * A_WIDTH // gw
    bg_col0 = (3 * A_WIDTH + B_WIDTH) // gw
    kv_col0 = (3 * A_WIDTH + 2 * B_WIDTH) // HEAD_DIM
    kv_srcs = [(z, kv_col0 + i * B_KV_HEADS) for i in range(6)]
    zg_col0 = n_main // LANES
    w1 = cmp_w1.reshape(2, CMP_LEN, HEAD_DIM, CMP_HIDDEN).astype(BF16)
    kc, vct, ks, vst, kw, vwt = _kv_prep(kv_srcs, batch, seq, k_gain, cmp_pos, w1,
                                         cmp_b1.reshape(2, 1, CMP_HIDDEN), cmp_w2.astype(BF16))
    b_out = _nsa_attn(z, z, kc, vct, ks, vst, kw, vwt, q_gain, batch, seq, q_col0, bg_col0, zg_col0)

    w_out = w_out.astype(BF16)
    return _matmul_residual(h, [a_out, b_out], [w_out[:A_WIDTH], w_out[A_WIDTH:]], next_gain)


def _odd_layer(h, hn, batch, seq, layer, norm_g, w_in, lb_logits, out_gain, w_out, next_gain):
    w_in = w_in.astype(BF16)
    z = _matmul(hn if hn is not None else _rms_norm(h, norm_g), w_in)
    o = _hgrn(z, lb_logits, out_gain, batch, seq, layer)
    return _matmul_residual(h, [o], [w_out.astype(BF16)], next_gain)


def kernel(x, even_norm, even_w_in, sgu_ln_g, sgu_ln_b, sgu_w, sgu_b, nsa_q_gain, nsa_k_gain, cmp_pos, cmp_w1, cmp_b1, cmp_w2, even_w_out, odd_norm, odd_w_in, hgrn_lb, hgrn_out_gain, odd_w_out):
    batch, seq, d = x.shape
    depth = hgrn_lb.shape[0]
    h = x.reshape(batch * seq, d)
    hn = None
    for layer in range(depth):
        next_gain = odd_norm[(layer + 1) // 2] if (layer + 1 < depth and layer % 2 == 0) else None
        if layer % 2 == 0:
            e = layer // 2
            h, hn = _even_layer(h, batch, seq, even_norm[e], even_w_in[e], sgu_ln_g[e], sgu_ln_b[e],
                                sgu_w[e], sgu_b[e], nsa_q_gain[e], nsa_k_gain[e], cmp_pos[e], cmp_w1[e],
                                cmp_b1[e], cmp_w2[e], even_w_out[e], next_gain)
        else:
            o = layer // 2
            h, hn = _odd_layer(h, hn, batch, seq, layer, odd_norm[o], odd_w_in[o], hgrn_lb,
                               hgrn_out_gain[o], odd_w_out[o], next_gain)
    return h.reshape(batch, seq, d)
```

```python
import functools

import numpy as np
import jax
import jax.numpy as jnp
from jax import lax
from jax.experimental import pallas as pl
from jax.experimental.pallas import tpu as pltpu

HEAD_DIM = 128
A_GROUPS = 8
A_WIDTH = A_GROUPS * HEAD_DIM
A_CHUNK = 128
B_HEADS = 8
B_KV_HEADS = 2
B_GROUP = B_HEADS // B_KV_HEADS
B_WIDTH = B_HEADS * HEAD_DIM
CMP_LEN = 32
CMP_STRIDE = 16
CMP_HIDDEN = 256
SLC_BLOCK = 64
TOP_N = 8
WINDOW = 512
C_HEADS = 16
C_DIM = 128
C_CHUNK = 64
EPS = 1e-6
NEG_INF = -1e30
FORCE_SCORE = 1e6

LANES = 128
BF16_SUBLANES = 16
VMEM_LIMIT_BYTES = 56 * 1024 * 1024

PROJ_TM = 1024
PLAIN_TN = 2304
RESID_TM = 512
RESID_TN = 2048
NORM_ROWS = 512
MIXA_ROWS = 512
ATT_TQ = 256
SLC_GROUP = 4
VT_ROWS = HEAD_DIM + BF16_SUBLANES
KEY_COLS = 2 * HEAD_DIM
ALIBI_SPLIT = 64
EXTRA_BLOCK0 = BF16_SUBLANES
HGRN_HEADS = 8
HGRN_ROWS = 1024

BF16 = jnp.bfloat16
F32 = jnp.float32


def _params(*sem):
    return pltpu.CompilerParams(dimension_semantics=sem, vmem_limit_bytes=VMEM_LIMIT_BYTES)


def _col_tile(n, limit):
    if n <= limit:
        return n
    return max(c for c in range(LANES, limit + 1, LANES) if n % c == 0)


def _dot(a, b):
    return jnp.dot(a, b, preferred_element_type=F32)


def _dot_nt(a, b):
    return lax.dot_general(a, b, (((1,), (1,)), ((), ())), preferred_element_type=F32)


def _dot_tn(a, b):
    return lax.dot_general(a, b, (((0,), (0,)), ((), ())), preferred_element_type=F32)


def _rms(x, gain):
    return x * lax.rsqrt(jnp.mean(x * x, axis=-1, keepdims=True) + EPS) * gain


def _rms_norm_kernel(x_ref, g_ref, o_ref):
    o_ref[...] = _rms(x_ref[...], g_ref[...]).astype(BF16)


def _rms_norm(x, gain):
    t, d = x.shape
    tm = min(NORM_ROWS, t)
    assert t % tm == 0
    return pl.pallas_call(
        _rms_norm_kernel,
        grid=(t // tm,),
        in_specs=[pl.BlockSpec((tm, d), lambda i: (i, 0)), pl.BlockSpec((1, d), lambda i: (0, 0))],
        out_specs=pl.BlockSpec((tm, d), lambda i: (i, 0)),
        out_shape=jax.ShapeDtypeStruct((t, d), BF16),
        compiler_params=_params("parallel"),
        name="rms_norm",
    )(x, gain.reshape(1, d))


def _matmul_residual_kernel(x_ref, *refs, n_pairs, emit_norm):
    acts, ws, rest = refs[:n_pairs], refs[n_pairs:2 * n_pairs], refs[2 * n_pairs:]
    acc = x_ref[...]
    for a_ref, w_ref in zip(acts, ws):
        acc = acc + _dot(a_ref[...], w_ref[...])
    if emit_norm:
        gain_ref, o_ref, xn_ref = rest
        xn_ref[...] = _rms(acc, gain_ref[...]).astype(BF16)
    else:
        (o_ref,) = rest
    o_ref[...] = acc


def _matmul_residual(x, acts, ws, next_gain=None):
    t, d = x.shape
    tm = min(RESID_TM, t)
    tn = _col_tile(d, RESID_TN)
    assert t % tm == 0 and d % tn == 0
    emit_norm = next_gain is not None and tn == d
    in_specs = [pl.BlockSpec((tm, tn), lambda i, j: (i, j))]
    in_specs += [pl.BlockSpec((tm, a.shape[1]), lambda i, j: (i, 0)) for a in acts]
    in_specs += [pl.BlockSpec((w.shape[0], tn), lambda i, j: (0, j)) for w in ws]
    out_specs = [pl.BlockSpec((tm, tn), lambda i, j: (i, j))]
    out_shape = [jax.ShapeDtypeStruct((t, d), F32)]
    args = [x, *acts, *ws]
    if emit_norm:
        in_specs.append(pl.BlockSpec((1, d), lambda i, j: (0, 0)))
        out_specs.append(pl.BlockSpec((tm, d), lambda i, j: (i, 0)))
        out_shape.append(jax.ShapeDtypeStruct((t, d), BF16))
        args.append(next_gain.reshape(1, d))
    out = pl.pallas_call(
        functools.partial(_matmul_residual_kernel, n_pairs=len(acts), emit_norm=emit_norm),
        grid=(t // tm, d // tn),
        in_specs=in_specs,
        out_specs=out_specs,
        out_shape=out_shape,
        compiler_params=_params("parallel", "arbitrary"),
        name="matmul_residual",
    )(*args)
    return (out[0], out[1]) if emit_norm else (out[0], None)


def _matmul_kernel(x_ref, w_ref, o_ref):
    o_ref[...] = _dot(x_ref[...], w_ref[...])


def _matmul(xn, w):
    t, d = xn.shape
    n = w.shape[1]
    tm = min(PROJ_TM, t)
    tn = _col_tile(n, PLAIN_TN)
    assert t % tm == 0 and n % tn == 0
    return pl.pallas_call(
        _matmul_kernel,
        grid=(t // tm, n // tn),
        in_specs=[pl.BlockSpec((tm, d), lambda i, j: (i, 0)), pl.BlockSpec((d, tn), lambda i, j: (0, j))],
        out_specs=pl.BlockSpec((tm, tn), lambda i, j: (i, j)),
        out_shape=jax.ShapeDtypeStruct((t, n), F32),
        compiler_params=_params("parallel", "arbitrary"),
        name="matmul",
    )(xn, w)


def _mixer_a_kernel(u_ref, v_ref, gate_ref, lng_ref, lnb_ref, w_ref, b_ref, o_ref):
    rows = u_ref.shape[0]
    v = jax.nn.gelu(v_ref[...])
    mu = jnp.mean(v, axis=-1, keepdims=True)
    vc = v - mu
    var = jnp.mean(vc * vc, axis=-1, keepdims=True)
    vn = (vc * lax.rsqrt(var + EPS) * lng_ref[...] + lnb_ref[...]).astype(BF16)
    tr = lax.broadcasted_iota(jnp.int32, (A_CHUNK, A_CHUNK), 0)
    tc = lax.broadcasted_iota(jnp.int32, (A_CHUNK, A_CHUNK), 1)
    for g in range(A_GROUPS):
        cols = slice(g * HEAD_DIM, (g + 1) * HEAD_DIM)
        wm = jnp.where(tr >= tc, w_ref[g], 0.0).astype(BF16)
        for c in range(rows // A_CHUNK):
            rs = slice(c * A_CHUNK, (c + 1) * A_CHUNK)
            sv = _dot(wm, vn[rs, cols]) + b_ref[g]
            u = jax.nn.gelu(u_ref[rs, cols])
            o_ref[rs, cols] = (u * sv * jax.nn.silu(gate_ref[rs, cols])).astype(BF16)


def _mixer_a(z, ln_g, ln_b, sgu_w, sgu_b_full):
    t = z.shape[0]
    rows = min(MIXA_ROWS, t)
    assert t % rows == 0 and rows % A_CHUNK == 0
    blk = lambda c: pl.BlockSpec((rows, A_WIDTH), lambda i: (i, c))
    full = lambda a: pl.BlockSpec(a.shape, lambda i: (0,) * a.ndim)
    lng = ln_g.reshape(1, A_WIDTH)
    lnb = ln_b.reshape(1, A_WIDTH)
    return pl.pallas_call(
        _mixer_a_kernel,
        grid=(t // rows,),
        in_specs=[blk(0), blk(1), blk(2), full(lng), full(lnb), full(sgu_w), full(sgu_b_full)],
        out_specs=pl.BlockSpec((rows, A_WIDTH), lambda i: (i, 0)),
        out_shape=jax.ShapeDtypeStruct((t, A_WIDTH), BF16),
        compiler_params=_params("parallel"),
        name="mixer_a",
    )(z, z, z, lng, lnb, sgu_w, sgu_b_full)


def _compress(src_ref, pos_ref, w1_ref, b1_ref, w2_ref):
    groups = src_ref.shape[0] // CMP_STRIDE
    lo = jnp.zeros((groups, CMP_HIDDEN), F32)
    hi = jnp.zeros((groups, CMP_HIDDEN), F32)
    for l in range(CMP_STRIDE):
        xl = src_ref[pl.ds(l, groups, stride=CMP_STRIDE), :]
        lo = lo + _dot((xl + pos_ref[l:l + 1, :]).astype(BF16), w1_ref[l])
        hi = hi + _dot((xl + pos_ref[CMP_STRIDE + l:CMP_STRIDE + l + 1, :]).astype(BF16),
                       w1_ref[CMP_STRIDE + l])
    hid = jax.nn.gelu(lo + pltpu.roll(hi, groups - 1, 0) + b1_ref[...])
    return _dot(hid.astype(BF16), w2_ref[...])


def _store_values_t(v_src, vt_ref):
    n_tiles, rows, tk = vt_ref.shape
    for kt in range(n_tiles):
        vt_ref[kt, :HEAD_DIM, :] = v_src[kt * tk:(kt + 1) * tk, :].T.astype(BF16)
        vt_ref[kt, HEAD_DIM:, :] = jnp.ones((rows - HEAD_DIM, tk), BF16)


def _key_extras(seq, with_blocks):
    kpos = lax.broadcasted_iota(jnp.int32, (seq, LANES), 0)
    c = lax.broadcasted_iota(jnp.int32, (seq, LANES), 1)
    hi = (kpos // ALIBI_SPLIT).astype(F32)
    lo = (kpos % ALIBI_SPLIT).astype(F32)
    ex = jnp.where(c == 0, hi, jnp.where(c == 1, lo, 0.0))
    if with_blocks:
        ex = jnp.where(kpos // SLC_BLOCK == c - EXTRA_BLOCK0, 1.0, ex)
    return ex.astype(BF16)


def _kv_prep_kernel(kc_src, vc_src, ks_src, vs_src, kw_src, vw_src, kgain_ref, pos_ref, w1_ref,
                    b1_ref, w2_ref, kc_ref, vct_ref, ks_ref, vst_ref, kw_ref, vwt_ref):
    seq = ks_src.shape[0]
    ks_ref[:, :HEAD_DIM] = _rms(ks_src[...], kgain_ref[1:2, :]).astype(BF16)
    ks_ref[:, HEAD_DIM:] = _key_extras(seq, True)
    kw_ref[:, :HEAD_DIM] = _rms(kw_src[...], kgain_ref[2:3, :]).astype(BF16)
    kw_ref[:, HEAD_DIM:] = _key_extras(seq, False)
    _store_values_t(vs_src, vst_ref)
    _store_values_t(vw_src, vwt_ref)
    kc = _compress(kc_src, pos_ref.at[0], w1_ref.at[0], b1_ref.at[0], w2_ref.at[0])
    kc_ref[...] = _rms(kc, kgain_ref[0:1, :]).astype(BF16)
    vc = _compress(vc_src, pos_ref.at[1], w1_ref.at[1], b1_ref.at[1], w2_ref.at[1])
    vct_ref[...] = vc.T.astype(BF16)


def _kv_prep(kv_srcs, batch, seq, k_gain, cmp_pos, w1, b1, w2):
    groups = seq // CMP_STRIDE
    tk = min(ATT_TQ, seq)
    src = lambda i: pl.BlockSpec((seq, HEAD_DIM), lambda b, h, c0=kv_srcs[i][1]: (b, c0 + h))
    full = lambda a: pl.BlockSpec(a.shape, lambda b, h: (0,) * a.ndim)
    spec4 = lambda r, c: pl.BlockSpec((None, None, r, c), lambda b, h: (b, h, 0, 0))
    shape4 = lambda r, c: jax.ShapeDtypeStruct((batch, B_KV_HEADS, r, c), BF16)
    vt_spec = pl.BlockSpec((None, None, seq // tk, VT_ROWS, tk), lambda b, h: (b, h, 0, 0, 0))
    vt_shape = jax.ShapeDtypeStruct((batch, B_KV_HEADS, seq // tk, VT_ROWS, tk), BF16)
    return pl.pallas_call(
        _kv_prep_kernel,
        grid=(batch, B_KV_HEADS),
        in_specs=[src(i) for i in range(6)] + [full(k_gain), full(cmp_pos), full(w1), full(b1), full(w2)],
        out_specs=[spec4(groups, HEAD_DIM), spec4(HEAD_DIM, groups), spec4(seq, KEY_COLS), vt_spec,
                   spec4(seq, KEY_COLS), vt_spec],
        out_shape=[shape4(groups, HEAD_DIM), shape4(HEAD_DIM, groups), shape4(seq, KEY_COLS), vt_shape,
                   shape4(seq, KEY_COLS), vt_shape],
        compiler_params=_params("parallel", "parallel"),
        name="kv_prep",
    )(*[a for a, _ in kv_srcs], k_gain, cmp_pos, w1, b1, w2)


def _nsa_kernel(slopes_ref, q_ref, bg_ref, gl_ref, kc_ref, vct_ref, ks_ref, vst_ref, kw_ref, vwt_ref,
                qgain_ref, ovt_ref, o_ref, qt_ref, m_ref, acc_ref, *, n_cmp):
    tq = q_ref.shape[0]
    n_grp = kc_ref.shape[0]
    n_slc = ovt_ref.shape[0]
    h = pl.program_id(1)
    i = pl.program_id(2)
    t0 = i * tq

    for g in range(B_GROUP):
        q = q_ref[:, g * HEAD_DIM:(g + 1) * HEAD_DIM]
        qt_ref[g, :HEAD_DIM, :] = (_rms(q, qgain_ref[...]) * (HEAD_DIM ** -0.5)).T.astype(BF16)

    n_idx = lax.broadcasted_iota(jnp.int32, (n_grp, tq), 0)
    t_idx = t0 + lax.broadcasted_iota(jnp.int32, (n_grp, tq), 1)
    dist_c = t_idx - (n_idx * CMP_STRIDE + (CMP_LEN - 1))
    valid_c = (n_idx < n_cmp) & (dist_c >= 0)
    dist_cf = dist_c.astype(F32)
    groups = range(B_GROUP)
    s_c = [_dot(kc_ref[...], qt_ref[g, :HEAD_DIM, :]) for g in groups]
    s_c = [jnp.where(valid_c, s - slopes_ref[h * B_GROUP + g] * dist_cf, NEG_INF) for g, s in zip(groups, s_c)]
    m_c = [jnp.max(s, axis=0, keepdims=True) for s in s_c]
    e_c = [jnp.where(valid_c, jnp.exp(s - m), 0.0) for s, m in zip(s_c, m_c)]
    p_c = [e / jnp.maximum(jnp.sum(e, axis=0, keepdims=True), 1e-30) for e in e_c]
    o_cmp = [_dot(vct_ref[...], p.astype(BF16)) for p in p_c]
    p_sum = sum(p_c[1:], p_c[0])

    imp = jnp.dot(ovt_ref[...], p_sum, precision=lax.Precision.HIGHEST, preferred_element_type=F32)
    j_idx = lax.broadcasted_iota(jnp.int32, (n_slc, tq), 0)
    j_f = j_idx.astype(F32)
    t_sel = t0 + lax.broadcasted_iota(jnp.int32, (n_slc, tq), 1)
    cur = t_sel // SLC_BLOCK
    forced = (j_idx == 0) | (j_idx == cur) | (j_idx == cur - 1)
    val = jnp.where(forced, FORCE_SCORE, jnp.where(j_idx * SLC_BLOCK <= t_sel, imp, -1.0))
    sel_t = jnp.zeros((n_slc, tq), F32)
    for _ in range(min(TOP_N, n_slc)):
        best = jnp.max(val, axis=0, keepdims=True)
        first = jnp.min(jnp.where(val == best, j_f, float(n_slc)), axis=0, keepdims=True)
        hit = j_f == first
        sel_t = jnp.where(hit, 1.0, sel_t)
        val = jnp.where(hit, -jnp.inf, val)

    unselected = (sel_t - 1.0) * (-NEG_INF)
    e_row = lax.broadcasted_iota(jnp.int32, (EXTRA_BLOCK0, tq), 0)
    pad = jnp.zeros((HEAD_DIM - EXTRA_BLOCK0 - n_slc, tq), F32)
    for g in range(B_GROUP):
        slope = slopes_ref[h * B_GROUP + g]
        alibi = jnp.where(e_row == 0, slope * ALIBI_SPLIT, jnp.where(e_row == 1, slope, 0.0))
        qt_ref[g, HEAD_DIM:, :] = jnp.concatenate([alibi, unselected, pad], axis=0).astype(BF16)

    key_r = lax.broadcasted_iota(jnp.int32, (tq, tq), 0)
    qry_c = lax.broadcasted_iota(jnp.int32, (tq, tq), 1)
    causal = key_r <= qry_c

    SLC, WIN = 0, 1
    branch_refs = {SLC: (ks_ref, vst_ref), WIN: (kw_ref, vwt_ref)}

    def tiles_update(tiles, first):
        heads = range(B_GROUP)
        s = []
        for br, kt, mask in tiles:
            k_blk = branch_refs[br][0][pl.ds(pl.multiple_of(kt * tq, tq), tq), :]
            s.append([_dot(k_blk, qt_ref[g]) for g in heads])
        s = [[sg if mask is None else jnp.where(mask, sg, NEG_INF) for sg in st]
             for st, (_, _, mask) in zip(s, tiles)]
        branches = sorted({br for br, _, _ in tiles})
        m_old, m_new = {}, {}
        for br in branches:
            for g in heads:
                m = None
                for st, (tb, _, _) in zip(s, tiles):
                    if tb == br:
                        tile_max = jnp.max(st[g], axis=0, keepdims=True)
                        m = tile_max if m is None else jnp.maximum(m, tile_max)
                if not first:
                    m_old[br, g] = m_ref[br * B_GROUP + g]
                    m = jnp.maximum(m_old[br, g], m)
                m_new[br, g] = m
        p = [[jnp.exp(st[g] - m_new[br, g]).astype(BF16) for g in heads] for st, (br, _, _) in zip(s, tiles)]
        pv = [[_dot(branch_refs[br][1][kt], pt[g]) for g in heads] for pt, (br, kt, _) in zip(p, tiles)]
        for br in branches:
            for g in heads:
                total = None
                for pvt, (tb, _, _) in zip(pv, tiles):
                    if tb == br:
                        total = pvt[g] if total is None else total + pvt[g]
                slot = br * B_GROUP + g
                if first:
                    acc_ref[slot] = total
                else:
                    acc_ref[slot] = jnp.exp(m_old[br, g] - m_new[br, g]) * acc_ref[slot] + total
                m_ref[slot] = m_new[br, g]

    def finish(br):
        return [acc_ref[br * B_GROUP + g, :HEAD_DIM, :] / acc_ref[br * B_GROUP + g, HEAD_DIM:HEAD_DIM + 1, :]
                for g in range(B_GROUP)]

    win_tiles = (WINDOW + tq - 1) // tq

    def win_tile(off):
        needs_mask = off * tq + tq - 1 >= WINDOW
        return (WIN, i - off, ((qry_c - key_r) + off * tq < WINDOW) if needs_mask else None)

    for count in range(win_tiles + 1):
        @pl.when(jnp.minimum(i, win_tiles) == count)
        def _(count=count):
            tiles_update([(SLC, i, causal), (WIN, i, causal)] + [win_tile(off) for off in range(1, count + 1)],
                         True)

    def slc_group(j, carry):
        tiles_update([(SLC, SLC_GROUP * j + u, None) for u in range(SLC_GROUP)], False)
        return carry

    lax.fori_loop(0, i // SLC_GROUP, slc_group, 0)
    for rest in range(1, SLC_GROUP):
        @pl.when(i % SLC_GROUP == rest)
        def _(rest=rest):
            first_tile = i // SLC_GROUP * SLC_GROUP
            tiles_update([(SLC, first_tile + u, None) for u in range(rest)], False)

    o_slc = finish(SLC)
    o_win = finish(WIN)

    gates_t = jax.nn.sigmoid(gl_ref[...].T)
    for g in range(B_GROUP):
        c = slice(g * HEAD_DIM, (g + 1) * HEAD_DIM)
        o_t = (gates_t[g:g + 1] * o_cmp[g]
               + gates_t[B_GROUP + g:B_GROUP + g + 1] * o_slc[g]
               + gates_t[2 * B_GROUP + g:2 * B_GROUP + g + 1] * o_win[g])
        o_ref[:, c] = (o_t.T * jax.nn.silu(bg_ref[:, c])).astype(BF16)


def _overlap_t(n_grp, n_cmp, n_slc):
    cs = np.arange(n_grp)[None, :] * CMP_STRIDE
    ss = np.arange(n_slc)[:, None] * SLC_BLOCK
    ov = np.clip(np.minimum(cs + CMP_LEN, ss + SLC_BLOCK) - np.maximum(cs, ss), 0, None) / CMP_LEN
    ov = ov * (np.arange(n_grp)[None, :] < n_cmp)
    return jnp.asarray(ov, dtype=F32)


def _nsa_attn(z, zg, kc, vct, ks, vst, kw, vwt, q_gain, batch, seq, q_col0, bg_col0, zg_col0):
    t = z.shape[0]
    tq = min(ATT_TQ, seq)
    assert seq % tq == 0 and tq % SLC_BLOCK == 0 and seq // SLC_BLOCK <= LANES
    nq = seq // tq
    n_grp = seq // CMP_STRIDE
    n_cmp = (seq - CMP_LEN) // CMP_STRIDE + 1
    n_slc = seq // SLC_BLOCK
    gw = B_GROUP * HEAD_DIM
    h_idx = jnp.arange(1, B_HEADS + 1, dtype=F32)
    slopes = jnp.exp2(-8.0 * h_idx / B_HEADS)
    ovt = _overlap_t(n_grp, n_cmp, n_slc)
    rows = lambda b, h, i: b * nq + i
    spec4 = lambda r, c: pl.BlockSpec((None, None, r, c), lambda b, h, i: (b, h, 0, 0))
    vt_spec = pl.BlockSpec((None, None, nq, VT_ROWS, tq), lambda b, h, i: (b, h, 0, 0, 0))
    full = lambda a: pl.BlockSpec(a.shape, lambda b, h, i: (0,) * a.ndim)
    qg = q_gain.reshape(1, HEAD_DIM)
    return pl.pallas_call(
        functools.partial(_nsa_kernel, n_cmp=n_cmp),
        grid=(batch, B_KV_HEADS, nq),
        in_specs=[
            pl.BlockSpec(memory_space=pltpu.SMEM),
            pl.BlockSpec((tq, gw), lambda b, h, i: (rows(b, h, i), q_col0 + h)),
            pl.BlockSpec((tq, gw), lambda b, h, i: (rows(b, h, i), bg_col0 + h)),
            pl.BlockSpec((tq, LANES), lambda b, h, i: (rows(b, h, i), zg_col0 + h)),
            spec4(n_grp, HEAD_DIM), spec4(HEAD_DIM, n_grp), spec4(seq, KEY_COLS), vt_spec,
            spec4(seq, KEY_COLS), vt_spec, full(qg), full(ovt),
        ],
        out_specs=pl.BlockSpec((tq, gw), lambda b, h, i: (rows(b, h, i), h)),
        out_shape=jax.ShapeDtypeStruct((t, B_WIDTH), BF16),
        scratch_shapes=[
            pltpu.VMEM((B_GROUP, KEY_COLS, tq), BF16),
            pltpu.VMEM((2 * B_GROUP, 1, tq), F32),
            pltpu.VMEM((2 * B_GROUP, VT_ROWS, tq), F32),
        ],
        compiler_params=_params("parallel", "parallel", "arbitrary"),
        name="nsa_attn",
    )(slopes, z, z, zg, kc, vct, ks, vst, kw, vwt, qg, ovt)


def _hgrn_kernel(q_ref, f_ref, i_ref, gate_ref, lb_ref, og_ref, o_ref, state_ref, *, layer):
    n_rows = q_ref.shape[0]
    n_heads = q_ref.shape[1] // C_DIM
    mid = C_CHUNK // 2 - 1

    logits = lb_ref[...]
    e = jnp.exp(logits - jnp.max(logits, axis=0, keepdims=True))
    p = e / jnp.sum(e, axis=0, keepdims=True)
    lb = jnp.sum(p[:layer + 1], axis=0, keepdims=True) - p[0:1]

    r_i = lax.broadcasted_iota(jnp.int32, (C_CHUNK, C_CHUNK), 0)
    c_i = lax.broadcasted_iota(jnp.int32, (C_CHUNK, C_CHUNK), 1)
    lower = r_i >= c_i
    tri = jnp.where(lower, 1.0, 0.0).astype(BF16)

    @pl.when(pl.program_id(2) == 0)
    def _():
        state_ref[...] = jnp.zeros_like(state_ref)

    def body(c, carry):
        r0 = pl.multiple_of(c * C_CHUNK, C_CHUNK)
        rows = pl.ds(r0, C_CHUNK)
        f = lb + (1.0 - lb) * jax.nn.sigmoid(f_ref[rows, :])
        k = 1.0 - f
        g = jnp.log(f)
        g_hi = g.astype(BF16)
        g_rest = g - g_hi.astype(F32)
        g_mid = g_rest.astype(BF16)
        g_lo = (g_rest - g_mid.astype(F32)).astype(BF16)
        gcum = _dot(tri, g_hi) + _dot(tri, g_mid) + _dot(tri, g_lo)
        g_mid = gcum[mid:mid + 1, :]
        g_end = gcum[C_CHUNK - 1:C_CHUNK, :]
        qa = q_ref[rows, :] * jnp.exp(gcum - g_mid)
        kb = k * jnp.exp(g_mid - gcum)
        q_in = (qa * jnp.exp(g_mid)).astype(BF16)
        k_st = (kb * jnp.exp(g_end - g_mid)).astype(BF16)
        qa = qa.astype(BF16)
        kb = kb.astype(BF16)
        v = i_ref[rows, :].astype(BF16)
        decay = jnp.exp(g_end)
        heads = range(n_heads)
        cs = [slice(hh * C_DIM, (hh + 1) * C_DIM) for hh in heads]
        attn = [_dot_nt(qa[:, c], kb[:, c]) for c in cs]
        state = [state_ref[hh] for hh in heads]
        inter = [_dot(q_in[:, c], s.astype(BF16)) for c, s in zip(cs, state)]
        update = [_dot_tn(k_st[:, c], v[:, c]) for c in cs]
        attn = [jnp.where(lower, a, 0.0).astype(BF16) for a in attn]
        o = [_dot(a, v[:, c]) + x for a, c, x in zip(attn, cs, inter)]
        for hh in heads:
            decay_rows = jnp.broadcast_to(decay[:, cs[hh]], (C_DIM, C_DIM)).T
            state_ref[hh] = decay_rows * state[hh] + update[hh]
        for hh in heads:
            out = _rms(o[hh], og_ref[...]) * jax.nn.silu(gate_ref[rows, cs[hh]])
            o_ref[rows, cs[hh]] = out.astype(BF16)
        return carry

    lax.fori_loop(0, n_rows // C_CHUNK, body, 0, unroll=8)


def _hgrn(z, lb_logits, out_gain, batch, seq, layer):
    t = z.shape[0]
    width = C_HEADS * C_DIM
    bw = HGRN_HEADS * C_DIM
    nb = width // bw
    rows = min(HGRN_ROWS, seq)
    ns = seq // rows
    assert seq % rows == 0 and rows % C_CHUNK == 0
    depth = lb_logits.shape[0]
    col = lambda part: pl.BlockSpec((rows, bw), lambda b, j, s, part=part: (b * ns + s, part * nb + j))
    og = out_gain.reshape(1, C_DIM)
    return pl.pallas_call(
        functools.partial(_hgrn_kernel, layer=layer),
        grid=(batch, nb, ns),
        in_specs=[col(0), col(1), col(2), col(3),
                  pl.BlockSpec((depth, bw), lambda b, j, s: (0, j)),
                  pl.BlockSpec((1, C_DIM), lambda b, j, s: (0, 0))],
        out_specs=pl.BlockSpec((rows, bw), lambda b, j, s: (b * ns + s, j)),
        out_shape=jax.ShapeDtypeStruct((t, width), BF16),
        scratch_shapes=[pltpu.VMEM((HGRN_HEADS, C_DIM, C_DIM), F32)],
        compiler_params=_params("parallel", "parallel", "arbitrary"),
        name="hgrn",
    )(z, z, z, z, lb_logits, og)


def _even_layer(h, batch, seq, norm_g, w_in, ln_g, ln_b, sgu_w, sgu_b, q_gain, k_gain,
                cmp_pos, cmp_w1, cmp_b1, cmp_w2, w_out, next_gain):
    d = h.shape[1]
    n_main = 3 * A_WIDTH + 2 * B_WIDTH + 6 * B_KV_HEADS * HEAD_DIM
    wg = w_in[:, n_main:].reshape(d, 3, B_KV_HEADS, B_GROUP).transpose(0, 2, 1, 3)
    wg = wg.reshape(d, B_KV_HEADS, 3 * B_GROUP)
    wg = jnp.pad(wg, ((0, 0), (0, 0), (0, LANES - 3 * B_GROUP))).reshape(d, B_KV_HEADS * LANES)
    w_all = jnp.concatenate([w_in[:, :n_main].astype(BF16), wg.astype(BF16)], axis=1)
    z = _matmul(_rms_norm(h, norm_g), w_all)

    sgu_b_full = jnp.broadcast_to(sgu_b[:, :, None], (A_GROUPS, A_CHUNK, HEAD_DIM))
    a_out = _mixer_a(z, ln_g, ln_b, sgu_w, sgu_b_full)

    gw = B_GROUP * HEAD_DIM
    q_col0 = 3 * A_WIDTH // gw
    bg_col0 = (3 * A_WIDTH + B_WIDTH) // gw
    kv_col0 = (3 * A_WIDTH + 2 * B_WIDTH) // HEAD_DIM
    kv_srcs = [(z, kv_col0 + i * B_KV_HEADS) for i in range(6)]
    zg_col0 = n_main // LANES
    w1 = cmp_w1.reshape(2, CMP_LEN, HEAD_DIM, CMP_HIDDEN).astype(BF16)
    kc, vct, ks, vst, kw, vwt = _kv_prep(kv_srcs, batch, seq, k_gain, cmp_pos, w1,
                                         cmp_b1.reshape(2, 1, CMP_HIDDEN), cmp_w2.astype(BF16))
    b_out = _nsa_attn(z, z, kc, vct, ks, vst, kw, vwt, q_gain, batch, seq, q_col0, bg_col0, zg_col0)

    w_out = w_out.astype(BF16)
    return _matmul_residual(h, [a_out, b_out], [w_out[:A_WIDTH], w_out[A_WIDTH:]], next_gain)


def _odd_layer(h, hn, batch, seq, layer, norm_g, w_in, lb_logits, out_gain, w_out, next_gain):
    w_in = w_in.astype(BF16)
    z = _matmul(hn if hn is not None else _rms_norm(h, norm_g), w_in)
    o = _hgrn(z, lb_logits, out_gain, batch, seq, layer)
    return _matmul_residual(h, [o], [w_out.astype(BF16)], next_gain)


def kernel(x, even_norm, even_w_in, sgu_ln_g, sgu_ln_b, sgu_w, sgu_b, nsa_q_gain, nsa_k_gain, cmp_pos, cmp_w1, cmp_b1, cmp_w2, even_w_out, odd_norm, odd_w_in, hgrn_lb, hgrn_out_gain, odd_w_out):
    batch, seq, d = x.shape
    depth = hgrn_lb.shape[0]
    h = x.reshape(batch * seq, d)
    hn = None
    for layer in range(depth):
        next_gain = odd_norm[(layer + 1) // 2] if (layer + 1 < depth and layer % 2 == 0) else None
        if layer % 2 == 0:
            e = layer // 2
            h, hn = _even_layer(h, batch, seq, even_norm[e], even_w_in[e], sgu_ln_g[e], sgu_ln_b[e],
                                sgu_w[e], sgu_b[e], nsa_q_gain[e], nsa_k_gain[e], cmp_pos[e], cmp_w1[e],
                                cmp_b1[e], cmp_w2[e], even_w_out[e], next_gain)
        else:
            o = layer // 2
            h, hn = _odd_layer(h, hn, batch, seq, layer, odd_norm[o], odd_w_in[o], hgrn_lb,
                               hgrn_out_gain[o], odd_w_out[o], next_gain)
    return h.reshape(batch, seq, d)
```

```python
import functools

import numpy as np
import jax
import jax.numpy as jnp
from jax import lax
from jax.experimental import pallas as pl
from jax.experimental.pallas import tpu as pltpu

HEAD_DIM = 128
A_GROUPS = 8
A_WIDTH = A_GROUPS * HEAD_DIM
A_CHUNK = 128
B_HEADS = 8
B_KV_HEADS = 2
B_GROUP = B_HEADS // B_KV_HEADS
B_WIDTH = B_HEADS * HEAD_DIM
CMP_LEN = 32
CMP_STRIDE = 16
CMP_HIDDEN = 256
SLC_BLOCK = 64
TOP_N = 8
WINDOW = 512
C_HEADS = 16
C_DIM = 128
C_CHUNK = 64
EPS = 1e-6
NEG_INF = -1e30
FORCE_SCORE = 1e6

LANES = 128
BF16_SUBLANES = 16
VMEM_LIMIT_BYTES = 56 * 1024 * 1024

PROJ_TM = 1024
PLAIN_TN = 2304
RESID_TM = 512
RESID_TN = 2048
NORM_ROWS = 512
MIXA_ROWS = 512
ATT_TQ = 256
SLC_GROUP = 4
VT_ROWS = HEAD_DIM + BF16_SUBLANES
KEY_COLS = 2 * HEAD_DIM
ALIBI_SPLIT = 64
EXTRA_BLOCK0 = BF16_SUBLANES
HGRN_HEADS = 8
HGRN_ROWS = 1024

BF16 = jnp.bfloat16
F32 = jnp.float32


def _params(*sem):
    return pltpu.CompilerParams(dimension_semantics=sem, vmem_limit_bytes=VMEM_LIMIT_BYTES)


def _col_tile(n, limit):
    if n <= limit:
        return n
    return max(c for c in range(LANES, limit + 1, LANES) if n % c == 0)


def _dot(a, b):
    return jnp.dot(a, b, preferred_element_type=F32)


def _dot_nt(a, b):
    return lax.dot_general(a, b, (((1,), (1,)), ((), ())), preferred_element_type=F32)


def _dot_tn(a, b):
    return lax.dot_general(a, b, (((0,), (0,)), ((), ())), preferred_element_type=F32)


def _rms(x, gain):
    return x * lax.rsqrt(jnp.mean(x * x, axis=-1, keepdims=True) + EPS) * gain


def _rms_norm_kernel(x_ref, g_ref, o_ref):
    o_ref[...] = _rms(x_ref[...], g_ref[...]).astype(BF16)


def _rms_norm(x, gain):
    t, d = x.shape
    tm = min(NORM_ROWS, t)
    assert t % tm == 0
    return pl.pallas_call(
        _rms_norm_kernel,
        grid=(t // tm,),
        in_specs=[pl.BlockSpec((tm, d), lambda i: (i, 0)), pl.BlockSpec((1, d), lambda i: (0, 0))],
        out_specs=pl.BlockSpec((tm, d), lambda i: (i, 0)),
        out_shape=jax.ShapeDtypeStruct((t, d), BF16),
        compiler_params=_params("parallel"),
        name="rms_norm",
    )(x, gain.reshape(1, d))


def _matmul_residual_kernel(x_ref, *refs, n_pairs, emit_norm):
    acts, ws, rest = refs[:n_pairs], refs[n_pairs:2 * n_pairs], refs[2 * n_pairs:]
    acc = x_ref[...]
    for a_ref, w_ref in zip(acts, ws):
        acc = acc + _dot(a_ref[...], w_ref[...])
    if emit_norm:
        gain_ref, o_ref, xn_ref = rest
        xn_ref[...] = _rms(acc, gain_ref[...]).astype(BF16)
    else:
        (o_ref,) = rest
    o_ref[...] = acc


def _matmul_residual(x, acts, ws, next_gain=None):
    t, d = x.shape
    tm = min(RESID_TM, t)
    tn = _col_tile(d, RESID_TN)
    assert t % tm == 0 and d % tn == 0
    emit_norm = next_gain is not None and tn == d
    in_specs = [pl.BlockSpec((tm, tn), lambda i, j: (i, j))]
    in_specs += [pl.BlockSpec((tm, a.shape[1]), lambda i, j: (i, 0)) for a in acts]
    in_specs += [pl.BlockSpec((w.shape[0], tn), lambda i, j: (0, j)) for w in ws]
    out_specs = [pl.BlockSpec((tm, tn), lambda i, j: (i, j))]
    out_shape = [jax.ShapeDtypeStruct((t, d), F32)]
    args = [x, *acts, *ws]
    if emit_norm:
        in_specs.append(pl.BlockSpec((1, d), lambda i, j: (0, 0)))
        out_specs.append(pl.BlockSpec((tm, d), lambda i, j: (i, 0)))
        out_shape.append(jax.ShapeDtypeStruct((t, d), BF16))
        args.append(next_gain.reshape(1, d))
    out = pl.pallas_call(
        functools.partial(_matmul_residual_kernel, n_pairs=len(acts), emit_norm=emit_norm),
        grid=(t // tm, d // tn),
        in_specs=in_specs,
        out_specs=out_specs,
        out_shape=out_shape,
        compiler_params=_params("parallel", "arbitrary"),
        name="matmul_residual",
    )(*args)
    return (out[0], out[1]) if emit_norm else (out[0], None)


def _matmul_kernel(x_ref, w_ref, o_ref):
    o_ref[...] = _dot(x_ref[...], w_ref[...])


def _matmul(xn, w):
    t, d = xn.shape
    n = w.shape[1]
    tm = min(PROJ_TM, t)
    tn = _col_tile(n, PLAIN_TN)
    assert t % tm == 0 and n % tn == 0
    return pl.pallas_call(
        _matmul_kernel,
        grid=(t // tm, n // tn),
        in_specs=[pl.BlockSpec((tm, d), lambda i, j: (i, 0)), pl.BlockSpec((d, tn), lambda i, j: (0, j))],
        out_specs=pl.BlockSpec((tm, tn), lambda i, j: (i, j)),
        out_shape=jax.ShapeDtypeStruct((t, n), F32),
        compiler_params=_params("parallel", "arbitrary"),
        name="matmul",
    )(xn, w)


def _mixer_a_kernel(u_ref, v_ref, gate_ref, lng_ref, lnb_ref, w_ref, b_ref, o_ref):
    rows = u_ref.shape[0]
    v = jax.nn.gelu(v_ref[...])
    mu = jnp.mean(v, axis=-1, keepdims=True)
    vc = v - mu
    var = jnp.mean(vc * vc, axis=-1, keepdims=True)
    vn = (vc * lax.rsqrt(var + EPS) * lng_ref[...] + lnb_ref[...]).astype(BF16)
    tr = lax.broadcasted_iota(jnp.int32, (A_CHUNK, A_CHUNK), 0)
    tc = lax.broadcasted_iota(jnp.int32, (A_CHUNK, A_CHUNK), 1)
    for g in range(A_GROUPS):
        cols = slice(g * HEAD_DIM, (g + 1) * HEAD_DIM)
        wm = jnp.where(tr >= tc, w_ref[g], 0.0).astype(BF16)
        for c in range(rows // A_CHUNK):
            rs = slice(c * A_CHUNK, (c + 1) * A_CHUNK)
            sv = _dot(wm, vn[rs, cols]) + b_ref[g]
            u = jax.nn.gelu(u_ref[rs, cols])
            o_ref[rs, cols] = (u * sv * jax.nn.silu(gate_ref[rs, cols])).astype(BF16)


def _mixer_a(z, ln_g, ln_b, sgu_w, sgu_b_full):
    t = z.shape[0]
    rows = min(MIXA_ROWS, t)
    assert t % rows == 0 and rows % A_CHUNK == 0
    blk = lambda c: pl.BlockSpec((rows, A_WIDTH), lambda i: (i, c))
    full = lambda a: pl.BlockSpec(a.shape, lambda i: (0,) * a.ndim)
    lng = ln_g.reshape(1, A_WIDTH)
    lnb = ln_b.reshape(1, A_WIDTH)
    return pl.pallas_call(
        _mixer_a_kernel,
        grid=(t // rows,),
        in_specs=[blk(0), blk(1), blk(2), full(lng), full(lnb), full(sgu_w), full(sgu_b_full)],
        out_specs=pl.BlockSpec((rows, A_WIDTH), lambda i: (i, 0)),
        out_shape=jax.ShapeDtypeStruct((t, A_WIDTH), BF16),
        compiler_params=_params("parallel"),
        name="mixer_a",
    )(z, z, z, lng, lnb, sgu_w, sgu_b_full)


def _compress(src_ref, pos_ref, w1_ref, b1_ref, w2_ref):
    groups = src_ref.shape[0] // CMP_STRIDE
    lo = jnp.zeros((groups, CMP_HIDDEN), F32)
    hi = jnp.zeros((groups, CMP_HIDDEN), F32)
    for l in range(CMP_STRIDE):
        xl = src_ref[pl.ds(l, groups, stride=CMP_STRIDE), :]
        lo = lo + _dot((xl + pos_ref[l:l + 1, :]).astype(BF16), w1_ref[l])
        hi = hi + _dot((xl + pos_ref[CMP_STRIDE + l:CMP_STRIDE + l + 1, :]).astype(BF16),
                       w1_ref[CMP_STRIDE + l])
    hid = jax.nn.gelu(lo + pltpu.roll(hi, groups - 1, 0) + b1_ref[...])
    return _dot(hid.astype(BF16), w2_ref[...])


def _store_values_t(v_src, vt_ref):
    n_tiles, rows, tk = vt_ref.shape
    for kt in range(n_tiles):
        vt_ref[kt, :HEAD_DIM, :] = v_src[kt * tk:(kt + 1) * tk, :].T.astype(BF16)
        vt_ref[kt, HEAD_DIM:, :] = jnp.ones((rows - HEAD_DIM, tk), BF16)


def _key_extras(seq, with_blocks):
    kpos = lax.broadcasted_iota(jnp.int32, (seq, LANES), 0)
    c = lax.broadcasted_iota(jnp.int32, (seq, LANES), 1)
    hi = (kpos // ALIBI_SPLIT).astype(F32)
    lo = (kpos % ALIBI_SPLIT).astype(F32)
    ex = jnp.where(c == 0, hi, jnp.where(c == 1, lo, 0.0))
    if with_blocks:
        ex = jnp.where(kpos // SLC_BLOCK == c - EXTRA_BLOCK0, 1.0, ex)
    return ex.astype(BF16)


def _kv_prep_kernel(kc_src, vc_src, ks_src, vs_src, kw_src, vw_src, kgain_ref, pos_ref, w1_ref,
                    b1_ref, w2_ref, kc_ref, vct_ref, ks_ref, vst_ref, kw_ref, vwt_ref):
    seq = ks_src.shape[0]
    ks_ref[:, :HEAD_DIM] = _rms(ks_src[...], kgain_ref[1:2, :]).astype(BF16)
    ks_ref[:, HEAD_DIM:] = _key_extras(seq, True)
    kw_ref[:, :HEAD_DIM] = _rms(kw_src[...], kgain_ref[2:3, :]).astype(BF16)
    kw_ref[:, HEAD_DIM:] = _key_extras(seq, False)
    _store_values_t(vs_src, vst_ref)
    _store_values_t(vw_src, vwt_ref)
    kc = _compress(kc_src, pos_ref.at[0], w1_ref.at[0], b1_ref.at[0], w2_ref.at[0])
    kc_ref[...] = _rms(kc, kgain_ref[0:1, :]).astype(BF16)
    vc = _compress(vc_src, pos_ref.at[1], w1_ref.at[1], b1_ref.at[1], w2_ref.at[1])
    vct_ref[...] = vc.T.astype(BF16)


def _kv_prep(kv_srcs, batch, seq, k_gain, cmp_pos, w1, b1, w2):
    groups = seq // CMP_STRIDE
    tk = min(ATT_TQ, seq)
    src = lambda i: pl.BlockSpec((seq, HEAD_DIM), lambda b, h, c0=kv_srcs[i][1]: (b, c0 + h))
    full = lambda a: pl.BlockSpec(a.shape, lambda b, h: (0,) * a.ndim)
    spec4 = lambda r, c: pl.BlockSpec((None, None, r, c), lambda b, h: (b, h, 0, 0))
    shape4 = lambda r, c: jax.ShapeDtypeStruct((batch, B_KV_HEADS, r, c), BF16)
    vt_spec = pl.BlockSpec((None, None, seq // tk, VT_ROWS, tk), lambda b, h: (b, h, 0, 0, 0))
    vt_shape = jax.ShapeDtypeStruct((batch, B_KV_HEADS, seq // tk, VT_ROWS, tk), BF16)
    return pl.pallas_call(
        _kv_prep_kernel,
        grid=(batch, B_KV_HEADS),
        in_specs=[src(i) for i in range(6)] + [full(k_gain), full(cmp_pos), full(w1), full(b1), full(w2)],
        out_specs=[spec4(groups, HEAD_DIM), spec4(HEAD_DIM, groups), spec4(seq, KEY_COLS), vt_spec,
                   spec4(seq, KEY_COLS), vt_spec],
        out_shape=[shape4(groups, HEAD_DIM), shape4(HEAD_DIM, groups), shape4(seq, KEY_COLS), vt_shape,
                   shape4(seq, KEY_COLS), vt_shape],
        compiler_params=_params("parallel", "parallel"),
        name="kv_prep",
    )(*[a for a, _ in kv_srcs], k_gain, cmp_pos, w1, b1, w2)


def _nsa_kernel(slopes_ref, q_ref, bg_ref, gl_ref, kc_ref, vct_ref, ks_ref, vst_ref, kw_ref, vwt_ref,
                qgain_ref, ovt_ref, o_ref, qt_ref, m_ref, acc_ref, *, n_cmp):
    tq = q_ref.shape[0]
    n_grp = kc_ref.shape[0]
    n_slc = ovt_ref.shape[0]
    h = pl.program_id(1)
    i = pl.program_id(2)
    t0 = i * tq

    for g in range(B_GROUP):
        q = q_ref[:, g * HEAD_DIM:(g + 1) * HEAD_DIM]
        qt_ref[g, :HEAD_DIM, :] = (_rms(q, qgain_ref[...]) * (HEAD_DIM ** -0.5)).T.astype(BF16)

    n_idx = lax.broadcasted_iota(jnp.int32, (n_grp, tq), 0)
    t_idx = t0 + lax.broadcasted_iota(jnp.int32, (n_grp, tq), 1)
    dist_c = t_idx - (n_idx * CMP_STRIDE + (CMP_LEN - 1))
    valid_c = (n_idx < n_cmp) & (dist_c >= 0)
    dist_cf = dist_c.astype(F32)
    groups = range(B_GROUP)
    s_c = [_dot(kc_ref[...], qt_ref[g, :HEAD_DIM, :]) for g in groups]
    s_c = [jnp.where(valid_c, s - slopes_ref[h * B_GROUP + g] * dist_cf, NEG_INF) for g, s in zip(groups, s_c)]
    m_c = [jnp.max(s, axis=0, keepdims=True) for s in s_c]
    e_c = [jnp.where(valid_c, jnp.exp(s - m), 0.0) for s, m in zip(s_c, m_c)]
    p_c = [e / jnp.maximum(jnp.sum(e, axis=0, keepdims=True), 1e-30) for e in e_c]
    o_cmp = [_dot(vct_ref[...], p.astype(BF16)) for p in p_c]
    p_sum = sum(p_c[1:], p_c[0])

    imp = jnp.dot(ovt_ref[...], p_sum, precision=lax.Precision.HIGHEST, preferred_element_type=F32)
    j_idx = lax.broadcasted_iota(jnp.int32, (n_slc, tq), 0)
    j_f = j_idx.astype(F32)
    t_sel = t0 + lax.broadcasted_iota(jnp.int32, (n_slc, tq), 1)
    cur = t_sel // SLC_BLOCK
    forced = (j_idx == 0) | (j_idx == cur) | (j_idx == cur - 1)
    val = jnp.where(forced, FORCE_SCORE, jnp.where(j_idx * SLC_BLOCK <= t_sel, imp, -1.0))
    sel_t = jnp.zeros((n_slc, tq), F32)
    for _ in range(min(TOP_N, n_slc)):
        best = jnp.max(val, axis=0, keepdims=True)
        first = jnp.min(jnp.where(val == best, j_f, float(n_slc)), axis=0, keepdims=True)
        hit = j_f == first
        sel_t = jnp.where(hit, 1.0, sel_t)
        val = jnp.where(hit, -jnp.inf, val)

    unselected = (sel_t - 1.0) * (-NEG_INF)
    e_row = lax.broadcasted_iota(jnp.int32, (EXTRA_BLOCK0, tq), 0)
    pad = jnp.zeros((HEAD_DIM - EXTRA_BLOCK0 - n_slc, tq), F32)
    for g in range(B_GROUP):
        slope = slopes_ref[h * B_GROUP + g]
        alibi = jnp.where(e_row == 0, slope * ALIBI_SPLIT, jnp.where(e_row == 1, slope, 0.0))
        qt_ref[g, HEAD_DIM:, :] = jnp.concatenate([alibi, unselected, pad], axis=0).astype(BF16)

    key_r = lax.broadcasted_iota(jnp.int32, (tq, tq), 0)
    qry_c = lax.broadcasted_iota(jnp.int32, (tq, tq), 1)
    causal = key_r <= qry_c

    SLC, WIN = 0, 1
    branch_refs = {SLC: (ks_ref, vst_ref), WIN: (kw_ref, vwt_ref)}

    def tiles_update(tiles, first):
        heads = range(B_GROUP)
        s = []
        for br, kt, mask in tiles:
            k_blk = branch_refs[br][0][pl.ds(pl.multiple_of(kt * tq, tq), tq), :]
            s.append([_dot(k_blk, qt_ref[g]) for g in heads])
        s = [[sg if mask is None else jnp.where(mask, sg, NEG_INF) for sg in st]
             for st, (_, _, mask) in zip(s, tiles)]
        branches = sorted({br for br, _, _ in tiles})
        m_old, m_new = {}, {}
        for br in branches:
            for g in heads:
                m = None
                for st, (tb, _, _) in zip(s, tiles):
                    if tb == br:
                        tile_max = jnp.max(st[g], axis=0, keepdims=True)
                        m = tile_max if m is None else jnp.maximum(m, tile_max)
                if not first:
                    m_old[br, g] = m_ref[br * B_GROUP + g]
                    m = jnp.maximum(m_old[br, g], m)
                m_new[br, g] = m
        p = [[jnp.exp(st[g] - m_new[br, g]).astype(BF16) for g in heads] for st, (br, _, _) in zip(s, tiles)]
        pv = [[_dot(branch_refs[br][1][kt], pt[g]) for g in heads] for pt, (br, kt, _) in zip(p, tiles)]
        for br in branches:
            for g in heads:
                total = None
                for pvt, (tb, _, _) in zip(pv, tiles):
                    if tb == br:
                        total = pvt[g] if total is None else total + pvt[g]
                slot = br * B_GROUP + g
                if first:
                    acc_ref[slot] = total
                else:
                    acc_ref[slot] = jnp.exp(m_old[br, g] - m_new[br, g]) * acc_ref[slot] + total
                m_ref[slot] = m_new[br, g]

    def finish(br):
        return [acc_ref[br * B_GROUP + g, :HEAD_DIM, :] / acc_ref[br * B_GROUP + g, HEAD_DIM:HEAD_DIM + 1, :]
                for g in range(B_GROUP)]

    win_tiles = (WINDOW + tq - 1) // tq

    def win_tile(off):
        needs_mask = off * tq + tq - 1 >= WINDOW
        return (WIN, i - off, ((qry_c - key_r) + off * tq < WINDOW) if needs_mask else None)

    for count in range(win_tiles + 1):
        @pl.when(jnp.minimum(i, win_tiles) == count)
        def _(count=count):
            tiles_update([(SLC, i, causal), (WIN, i, causal)] + [win_tile(off) for off in range(1, count + 1)],
                         True)

    def slc_group(j, carry):
        tiles_update([(SLC, SLC_GROUP * j + u, None) for u in range(SLC_GROUP)], False)
        return carry

    lax.fori_loop(0, i // SLC_GROUP, slc_group, 0)
    for rest in range(1, SLC_GROUP):
        @pl.when(i % SLC_GROUP == rest)
        def _(rest=rest):
            first_tile = i // SLC_GROUP * SLC_GROUP
            tiles_update([(SLC, first_tile + u, None) for u in range(rest)], False)

    o_slc = finish(SLC)
    o_win = finish(WIN)

    gates_t = jax.nn.sigmoid(gl_ref[...].T)
    for g in range(B_GROUP):
        c = slice(g * HEAD_DIM, (g + 1) * HEAD_DIM)
        o_t = (gates_t[g:g + 1] * o_cmp[g]
               + gates_t[B_GROUP + g:B_GROUP + g + 1] * o_slc[g]
               + gates_t[2 * B_GROUP + g:2 * B_GROUP + g + 1] * o_win[g])
        o_ref[:, c] = (o_t.T * jax.nn.silu(bg_ref[:, c])).astype(BF16)


def _overlap_t(n_grp, n_cmp, n_slc):
    cs = np.arange(n_grp)[None, :] * CMP_STRIDE
    ss = np.arange(n_slc)[:, None] * SLC_BLOCK
    ov = np.clip(np.minimum(cs + CMP_LEN, ss + SLC_BLOCK) - np.maximum(cs, ss), 0, None) / CMP_LEN
    ov = ov * (np.arange(n_grp)[None, :] < n_cmp)
    return jnp.asarray(ov, dtype=F32)


def _nsa_attn(z, zg, kc, vct, ks, vst, kw, vwt, q_gain, batch, seq, q_col0, bg_col0, zg_col0):
    t = z.shape[0]
    tq = min(ATT_TQ, seq)
    assert seq % tq == 0 and tq % SLC_BLOCK == 0 and seq // SLC_BLOCK <= LANES
    nq = seq // tq
    n_grp = seq // CMP_STRIDE
    n_cmp = (seq - CMP_LEN) // CMP_STRIDE + 1
    n_slc = seq // SLC_BLOCK
    gw = B_GROUP * HEAD_DIM
    h_idx = jnp.arange(1, B_HEADS + 1, dtype=F32)
    slopes = jnp.exp2(-8.0 * h_idx / B_HEADS)
    ovt = _overlap_t(n_grp, n_cmp, n_slc)
    rows = lambda b, h, i: b * nq + i
    spec4 = lambda r, c: pl.BlockSpec((None, None, r, c), lambda b, h, i: (b, h, 0, 0))
    vt_spec = pl.BlockSpec((None, None, nq, VT_ROWS, tq), lambda b, h, i: (b, h, 0, 0, 0))
    full = lambda a: pl.BlockSpec(a.shape, lambda b, h, i: (0,) * a.ndim)
    qg = q_gain.reshape(1, HEAD_DIM)
    return pl.pallas_call(
        functools.partial(_nsa_kernel, n_cmp=n_cmp),
        grid=(batch, B_KV_HEADS, nq),
        in_specs=[
            pl.BlockSpec(memory_space=pltpu.SMEM),
            pl.BlockSpec((tq, gw), lambda b, h, i: (rows(b, h, i), q_col0 + h)),
            pl.BlockSpec((tq, gw), lambda b, h, i: (rows(b, h, i), bg_col0 + h)),
            pl.BlockSpec((tq, LANES), lambda b, h, i: (rows(b, h, i), zg_col0 + h)),
            spec4(n_grp, HEAD_DIM), spec4(HEAD_DIM, n_grp), spec4(seq, KEY_COLS), vt_spec,
            spec4(seq, KEY_COLS), vt_spec, full(qg), full(ovt),
        ],
        out_specs=pl.BlockSpec((tq, gw), lambda b, h, i: (rows(b, h, i), h)),
        out_shape=jax.ShapeDtypeStruct((t, B_WIDTH), BF16),
        scratch_shapes=[
            pltpu.VMEM((B_GROUP, KEY_COLS, tq), BF16),
            pltpu.VMEM((2 * B_GROUP, 1, tq), F32),
            pltpu.VMEM((2 * B_GROUP, VT_ROWS, tq), F32),
        ],
        compiler_params=_params("parallel", "parallel", "arbitrary"),
        name="nsa_attn",
    )(slopes, z, z, zg, kc, vct, ks, vst, kw, vwt, qg, ovt)


def _hgrn_kernel(q_ref, f_ref, i_ref, gate_ref, lb_ref, og_ref, o_ref, state_ref, *, layer):
    n_rows = q_ref.shape[0]
    n_heads = q_ref.shape[1] // C_DIM
    mid = C_CHUNK // 2 - 1

    logits = lb_ref[...]
    e = jnp.exp(logits - jnp.max(logits, axis=0, keepdims=True))
    p = e / jnp.sum(e, axis=0, keepdims=True)
    lb = jnp.sum(p[:layer + 1], axis=0, keepdims=True) - p[0:1]

    r_i = lax.broadcasted_iota(jnp.int32, (C_CHUNK, C_CHUNK), 0)
    c_i = lax.broadcasted_iota(jnp.int32, (C_CHUNK, C_CHUNK), 1)
    lower = r_i >= c_i
    tri = jnp.where(lower, 1.0, 0.0).astype(BF16)

    @pl.when(pl.program_id(2) == 0)
    def _():
        state_ref[...] = jnp.zeros_like(state_ref)

    def body(c, carry):
        r0 = pl.multiple_of(c * C_CHUNK, C_CHUNK)
        rows = pl.ds(r0, C_CHUNK)
        f = lb + (1.0 - lb) * jax.nn.sigmoid(f_ref[rows, :])
        k = 1.0 - f
        g = jnp.log(f)
        g_hi = g.astype(BF16)
        g_rest = g - g_hi.astype(F32)
        g_mid = g_rest.astype(BF16)
        g_lo = (g_rest - g_mid.astype(F32)).astype(BF16)
        gcum = _dot(tri, g_hi) + _dot(tri, g_mid) + _dot(tri, g_lo)
        g_mid = gcum[mid:mid + 1, :]
        g_end = gcum[C_CHUNK - 1:C_CHUNK, :]
        qa = q_ref[rows, :] * jnp.exp(gcum - g_mid)
        kb = k * jnp.exp(g_mid - gcum)
        q_in = (qa * jnp.exp(g_mid)).astype(BF16)
        k_st = (kb * jnp.exp(g_end - g_mid)).astype(BF16)
        qa = qa.astype(BF16)
        kb = kb.astype(BF16)
        v = i_ref[rows, :].astype(BF16)
        decay = jnp.exp(g_end)
        heads = range(n_heads)
        cs = [slice(hh * C_DIM, (hh + 1) * C_DIM) for hh in heads]
        attn = [_dot_nt(qa[:, c], kb[:, c]) for c in cs]
        state = [state_ref[hh] for hh in heads]
        inter = [_dot(q_in[:, c], s.astype(BF16)) for c, s in zip(cs, state)]
        update = [_dot_tn(k_st[:, c], v[:, c]) for c in cs]
        attn = [jnp.where(lower, a, 0.0).astype(BF16) for a in attn]
        o = [_dot(a, v[:, c]) + x for a, c, x in zip(attn, cs, inter)]
        for hh in heads:
            decay_rows = jnp.broadcast_to(decay[:, cs[hh]], (C_DIM, C_DIM)).T
            state_ref[hh] = decay_rows * state[hh] + update[hh]
        for hh in heads:
            out = _rms(o[hh], og_ref[...]) * jax.nn.silu(gate_ref[rows, cs[hh]])
            o_ref[rows, cs[hh]] = out.astype(BF16)
        return carry

    lax.fori_loop(0, n_rows // C_CHUNK, body, 0, unroll=16)


def _hgrn(z, lb_logits, out_gain, batch, seq, layer):
    t = z.shape[0]
    width = C_HEADS * C_DIM
    bw = HGRN_HEADS * C_DIM
    nb = width // bw
    rows = min(HGRN_ROWS, seq)
    ns = seq // rows
    assert seq % rows == 0 and rows % C_CHUNK == 0
    depth = lb_logits.shape[0]
    col = lambda part: pl.BlockSpec((rows, bw), lambda b, j, s, part=part: (b * ns + s, part * nb + j))
    og = out_gain.reshape(1, C_DIM)
    return pl.pallas_call(
        functools.partial(_hgrn_kernel, layer=layer),
        grid=(batch, nb, ns),
        in_specs=[col(0), col(1), col(2), col(3),
                  pl.BlockSpec((depth, bw), lambda b, j, s: (0, j)),
                  pl.BlockSpec((1, C_DIM), lambda b, j, s: (0, 0))],
        out_specs=pl.BlockSpec((rows, bw), lambda b, j, s: (b * ns + s, j)),
        out_shape=jax.ShapeDtypeStruct((t, width), BF16),
        scratch_shapes=[pltpu.VMEM((HGRN_HEADS, C_DIM, C_DIM), F32)],
        compiler_params=_params("parallel", "parallel", "arbitrary"),
        name="hgrn",
    )(z, z, z, z, lb_logits, og)


def _even_layer(h, batch, seq, norm_g, w_in, ln_g, ln_b, sgu_w, sgu_b, q_gain, k_gain,
                cmp_pos, cmp_w1, cmp_b1, cmp_w2, w_out, next_gain):
    d = h.shape[1]
    n_main = 3 * A_WIDTH + 2 * B_WIDTH + 6 * B_KV_HEADS * HEAD_DIM
    wg = w_in[:, n_main:].reshape(d, 3, B_KV_HEADS, B_GROUP).transpose(0, 2, 1, 3)
    wg = wg.reshape(d, B_KV_HEADS, 3 * B_GROUP)
    wg = jnp.pad(wg, ((0, 0), (0, 0), (0, LANES - 3 * B_GROUP))).reshape(d, B_KV_HEADS * LANES)
    w_all = jnp.concatenate([w_in[:, :n_main].astype(BF16), wg.astype(BF16)], axis=1)
    z = _matmul(_rms_norm(h, norm_g), w_all)

    sgu_b_full = jnp.broadcast_to(sgu_b[:, :, None], (A_GROUPS, A_CHUNK, HEAD_DIM))
    a_out = _mixer_a(z, ln_g, ln_b, sgu_w, sgu_b_full)

    gw = B_GROUP * HEAD_DIM
    q_col0 = 3 * A_WIDTH // gw
    bg_col0 = (3 * A_WIDTH + B_WIDTH) // gw
    kv_col0 = (3 * A_WIDTH + 2 * B_WIDTH) // HEAD_DIM
    kv_srcs = [(z, kv_col0 + i * B_KV_HEADS) for i in range(6)]
    zg_col0 = n_main // LANES
    w1 = cmp_w1.reshape(2, CMP_LEN, HEAD_DIM, CMP_HIDDEN).astype(BF16)
    kc, vct, ks, vst, kw, vwt = _kv_prep(kv_srcs, batch, seq, k_gain, cmp_pos, w1,
                                         cmp_b1.reshape(2, 1, CMP_HIDDEN), cmp_w2.astype(BF16))
    b_out = _nsa_attn(z, z, kc, vct, ks, vst, kw, vwt, q_gain, batch, seq, q_col0, bg_col0, zg_col0)

    w_out = w_out.astype(BF16)
    return _matmul_residual(h, [a_out, b_out], [w_out[:A_WIDTH], w_out[A_WIDTH:]], next_gain)


def _odd_layer(h, hn, batch, seq, layer, norm_g, w_in, lb_logits, out_gain, w_out, next_gain):
    w_in = w_in.astype(BF16)
    z = _matmul(hn if hn is not None else _rms_norm(h, norm_g), w_in)
    o = _hgrn(z, lb_logits, out_gain, batch, seq, layer)
    return _matmul_residual(h, [o], [w_out.astype(BF16)], next_gain)


def kernel(x, even_norm, even_w_in, sgu_ln_g, sgu_ln_b, sgu_w, sgu_b, nsa_q_gain, nsa_k_gain, cmp_pos, cmp_w1, cmp_b1, cmp_w2, even_w_out, odd_norm, odd_w_in, hgrn_lb, hgrn_out_gain, odd_w_out):
    batch, seq, d = x.shape
    depth = hgrn_lb.shape[0]
    h = x.reshape(batch * seq, d)
    hn = None
    for layer in range(depth):
        next_gain = odd_norm[(layer + 1) // 2] if (layer + 1 < depth and layer % 2 == 0) else None
        if layer % 2 == 0:
            e = layer // 2
            h, hn = _even_layer(h, batch, seq, even_norm[e], even_w_in[e], sgu_ln_g[e], sgu_ln_b[e],
                                sgu_w[e], sgu_b[e], nsa_q_gain[e], nsa_k_gain[e], cmp_pos[e], cmp_w1[e],
                                cmp_b1[e], cmp_w2[e], even_w_out[e], next_gain)
        else:
            o = layer // 2
            h, hn = _odd_layer(h, hn, batch, seq, layer, odd_norm[o], odd_w_in[o], hgrn_lb,
                               hgrn_out_gain[o], odd_w_out[o], next_gain)
    return h.reshape(batch, seq, d)
```

```python
import functools

import numpy as np
import jax
import jax.numpy as jnp
from jax import lax
from jax.experimental import pallas as pl
from jax.experimental.pallas import tpu as pltpu

HEAD_DIM = 128
A_GROUPS = 8
A_WIDTH = A_GROUPS * HEAD_DIM
A_CHUNK = 128
B_HEADS = 8
B_KV_HEADS = 2
B_GROUP = B_HEADS // B_KV_HEADS
B_WIDTH = B_HEADS * HEAD_DIM
CMP_LEN = 32
CMP_STRIDE = 16
CMP_HIDDEN = 256
SLC_BLOCK = 64
TOP_N = 8
WINDOW = 512
C_HEADS = 16
C_DIM = 128
C_CHUNK = 64
EPS = 1e-6
NEG_INF = -1e30
FORCE_SCORE = 1e6

LANES = 128
BF16_SUBLANES = 16
VMEM_LIMIT_BYTES = 56 * 1024 * 1024

PROJ_TM = 1024
PLAIN_TN = 2304
RESID_TM = 512
RESID_TN = 2048
NORM_ROWS = 512
MIXA_ROWS = 512
ATT_TQ = 256
SLC_GROUP = 4
VT_ROWS = HEAD_DIM + BF16_SUBLANES
KEY_COLS = 2 * HEAD_DIM
ALIBI_SPLIT = 64
EXTRA_BLOCK0 = BF16_SUBLANES
HGRN_HEADS = 8
HGRN_ROWS = 1024

BF16 = jnp.bfloat16
F32 = jnp.float32


def _params(*sem):
    return pltpu.CompilerParams(dimension_semantics=sem, vmem_limit_bytes=VMEM_LIMIT_BYTES)


def _col_tile(n, limit):
    if n <= limit:
        return n
    return max(c for c in range(LANES, limit + 1, LANES) if n % c == 0)


def _dot(a, b):
    return jnp.dot(a, b, preferred_element_type=F32)


def _dot_nt(a, b):
    return lax.dot_general(a, b, (((1,), (1,)), ((), ())), preferred_element_type=F32)


def _dot_tn(a, b):
    return lax.dot_general(a, b, (((0,), (0,)), ((), ())), preferred_element_type=F32)


def _gelu(x):
    k = -2.0 * np.sqrt(2.0 / np.pi) * np.log2(np.e)
    return x / (1.0 + jnp.exp2(x * (k + (k * 0.044715) * (x * x))))


def _rms(x, gain):
    return x * lax.rsqrt(jnp.mean(x * x, axis=-1, keepdims=True) + EPS) * gain


def _rms_norm_kernel(x_ref, g_ref, o_ref):
    o_ref[...] = _rms(x_ref[...], g_ref[...]).astype(BF16)


def _rms_norm(x, gain):
    t, d = x.shape
    tm = min(NORM_ROWS, t)
    assert t % tm == 0
    return pl.pallas_call(
        _rms_norm_kernel,
        grid=(t // tm,),
        in_specs=[pl.BlockSpec((tm, d), lambda i: (i, 0)), pl.BlockSpec((1, d), lambda i: (0, 0))],
        out_specs=pl.BlockSpec((tm, d), lambda i: (i, 0)),
        out_shape=jax.ShapeDtypeStruct((t, d), BF16),
        compiler_params=_params("parallel"),
        name="rms_norm",
    )(x, gain.reshape(1, d))


def _matmul_residual_kernel(x_ref, *refs, n_pairs, emit_norm):
    acts, ws, rest = refs[:n_pairs], refs[n_pairs:2 * n_pairs], refs[2 * n_pairs:]
    acc = x_ref[...]
    for a_ref, w_ref in zip(acts, ws):
        acc = acc + _dot(a_ref[...], w_ref[...])
    if emit_norm:
        gain_ref, o_ref, xn_ref = rest
        xn_ref[...] = _rms(acc, gain_ref[...]).astype(BF16)
    else:
        (o_ref,) = rest
    o_ref[...] = acc


def _matmul_residual(x, acts, ws, next_gain=None):
    t, d = x.shape
    tm = min(RESID_TM, t)
    tn = _col_tile(d, RESID_TN)
    assert t % tm == 0 and d % tn == 0
    emit_norm = next_gain is not None and tn == d
    in_specs = [pl.BlockSpec((tm, tn), lambda i, j: (i, j))]
    in_specs += [pl.BlockSpec((tm, a.shape[1]), lambda i, j: (i, 0)) for a in acts]
    in_specs += [pl.BlockSpec((w.shape[0], tn), lambda i, j: (0, j)) for w in ws]
    out_specs = [pl.BlockSpec((tm, tn), lambda i, j: (i, j))]
    out_shape = [jax.ShapeDtypeStruct((t, d), F32)]
    args = [x, *acts, *ws]
    if emit_norm:
        in_specs.append(pl.BlockSpec((1, d), lambda i, j: (0, 0)))
        out_specs.append(pl.BlockSpec((tm, d), lambda i, j: (i, 0)))
        out_shape.append(jax.ShapeDtypeStruct((t, d), BF16))
        args.append(next_gain.reshape(1, d))
    out = pl.pallas_call(
        functools.partial(_matmul_residual_kernel, n_pairs=len(acts), emit_norm=emit_norm),
        grid=(t // tm, d // tn),
        in_specs=in_specs,
        out_specs=out_specs,
        out_shape=out_shape,
        compiler_params=_params("parallel", "arbitrary"),
        name="matmul_residual",
    )(*args)
    return (out[0], out[1]) if emit_norm else (out[0], None)


def _matmul_kernel(x_ref, w_ref, o_ref):
    o_ref[...] = _dot(x_ref[...], w_ref[...])


def _matmul(xn, w):
    t, d = xn.shape
    n = w.shape[1]
    tm = min(PROJ_TM, t)
    tn = _col_tile(n, PLAIN_TN)
    assert t % tm == 0 and n % tn == 0
    return pl.pallas_call(
        _matmul_kernel,
        grid=(t // tm, n // tn),
        in_specs=[pl.BlockSpec((tm, d), lambda i, j: (i, 0)), pl.BlockSpec((d, tn), lambda i, j: (0, j))],
        out_specs=pl.BlockSpec((tm, tn), lambda i, j: (i, j)),
        out_shape=jax.ShapeDtypeStruct((t, n), F32),
        compiler_params=_params("parallel", "arbitrary"),
        name="matmul",
    )(xn, w)


def _mixer_a_kernel(u_ref, v_ref, gate_ref, lng_ref, lnb_ref, w_ref, b_ref, o_ref):
    rows = u_ref.shape[0]
    v = _gelu(v_ref[...])
    mu = jnp.mean(v, axis=-1, keepdims=True)
    vc = v - mu
    var = jnp.mean(vc * vc, axis=-1, keepdims=True)
    vn = (vc * lax.rsqrt(var + EPS) * lng_ref[...] + lnb_ref[...]).astype(BF16)
    tr = lax.broadcasted_iota(jnp.int32, (A_CHUNK, A_CHUNK), 0)
    tc = lax.broadcasted_iota(jnp.int32, (A_CHUNK, A_CHUNK), 1)
    for g in range(A_GROUPS):
        cols = slice(g * HEAD_DIM, (g + 1) * HEAD_DIM)
        wm = jnp.where(tr >= tc, w_ref[g], 0.0).astype(BF16)
        for c in range(rows // A_CHUNK):
            rs = slice(c * A_CHUNK, (c + 1) * A_CHUNK)
            sv = _dot(wm, vn[rs, cols]) + b_ref[g]
            u = _gelu(u_ref[rs, cols])
            o_ref[rs, cols] = (u * sv * jax.nn.silu(gate_ref[rs, cols])).astype(BF16)


def _mixer_a(z, ln_g, ln_b, sgu_w, sgu_b_full):
    t = z.shape[0]
    rows = min(MIXA_ROWS, t)
    assert t % rows == 0 and rows % A_CHUNK == 0
    blk = lambda c: pl.BlockSpec((rows, A_WIDTH), lambda i: (i, c))
    full = lambda a: pl.BlockSpec(a.shape, lambda i: (0,) * a.ndim)
    lng = ln_g.reshape(1, A_WIDTH)
    lnb = ln_b.reshape(1, A_WIDTH)
    return pl.pallas_call(
        _mixer_a_kernel,
        grid=(t // rows,),
        in_specs=[blk(0), blk(1), blk(2), full(lng), full(lnb), full(sgu_w), full(sgu_b_full)],
        out_specs=pl.BlockSpec((rows, A_WIDTH), lambda i: (i, 0)),
        out_shape=jax.ShapeDtypeStruct((t, A_WIDTH), BF16),
        compiler_params=_params("parallel"),
        name="mixer_a",
    )(z, z, z, lng, lnb, sgu_w, sgu_b_full)


def _compress(src_ref, pos_ref, w1_ref, b1_ref, w2_ref):
    groups = src_ref.shape[0] // CMP_STRIDE
    lo = jnp.zeros((groups, CMP_HIDDEN), F32)
    hi = jnp.zeros((groups, CMP_HIDDEN), F32)
    for l in range(CMP_STRIDE):
        xl = src_ref[pl.ds(l, groups, stride=CMP_STRIDE), :]
        lo = lo + _dot((xl + pos_ref[l:l + 1, :]).astype(BF16), w1_ref[l])
        hi = hi + _dot((xl + pos_ref[CMP_STRIDE + l:CMP_STRIDE + l + 1, :]).astype(BF16),
                       w1_ref[CMP_STRIDE + l])
    hid = _gelu(lo + pltpu.roll(hi, groups - 1, 0) + b1_ref[...])
    return _dot(hid.astype(BF16), w2_ref[...])


def _store_values_t(v_src, vt_ref):
    n_tiles, rows, tk = vt_ref.shape
    for kt in range(n_tiles):
        vt_ref[kt, :HEAD_DIM, :] = v_src[kt * tk:(kt + 1) * tk, :].T.astype(BF16)
        vt_ref[kt, HEAD_DIM:, :] = jnp.ones((rows - HEAD_DIM, tk), BF16)


def _key_extras(seq, with_blocks):
    kpos = lax.broadcasted_iota(jnp.int32, (seq, LANES), 0)
    c = lax.broadcasted_iota(jnp.int32, (seq, LANES), 1)
    hi = (kpos // ALIBI_SPLIT).astype(F32)
    lo = (kpos % ALIBI_SPLIT).astype(F32)
    ex = jnp.where(c == 0, hi, jnp.where(c == 1, lo, 0.0))
    if with_blocks:
        ex = jnp.where(kpos // SLC_BLOCK == c - EXTRA_BLOCK0, 1.0, ex)
    return ex.astype(BF16)


def _kv_prep_kernel(kc_src, vc_src, ks_src, vs_src, kw_src, vw_src, kgain_ref, pos_ref, w1_ref,
                    b1_ref, w2_ref, kc_ref, vct_ref, ks_ref, vst_ref, kw_ref, vwt_ref):
    seq = ks_src.shape[0]
    ks_ref[:, :HEAD_DIM] = _rms(ks_src[...], kgain_ref[1:2, :]).astype(BF16)
    ks_ref[:, HEAD_DIM:] = _key_extras(seq, True)
    kw_ref[:, :HEAD_DIM] = _rms(kw_src[...], kgain_ref[2:3, :]).astype(BF16)
    kw_ref[:, HEAD_DIM:] = _key_extras(seq, False)
    _store_values_t(vs_src, vst_ref)
    _store_values_t(vw_src, vwt_ref)
    kc = _compress(kc_src, pos_ref.at[0], w1_ref.at[0], b1_ref.at[0], w2_ref.at[0])
    kc_ref[...] = _rms(kc, kgain_ref[0:1, :]).astype(BF16)
    vc = _compress(vc_src, pos_ref.at[1], w1_ref.at[1], b1_ref.at[1], w2_ref.at[1])
    vct_ref[...] = vc.T.astype(BF16)


def _kv_prep(kv_srcs, batch, seq, k_gain, cmp_pos, w1, b1, w2):
    groups = seq // CMP_STRIDE
    tk = min(ATT_TQ, seq)
    src = lambda i: pl.BlockSpec((seq, HEAD_DIM), lambda b, h, c0=kv_srcs[i][1]: (b, c0 + h))
    full = lambda a: pl.BlockSpec(a.shape, lambda b, h: (0,) * a.ndim)
    spec4 = lambda r, c: pl.BlockSpec((None, None, r, c), lambda b, h: (b, h, 0, 0))
    shape4 = lambda r, c: jax.ShapeDtypeStruct((batch, B_KV_HEADS, r, c), BF16)
    vt_spec = pl.BlockSpec((None, None, seq // tk, VT_ROWS, tk), lambda b, h: (b, h, 0, 0, 0))
    vt_shape = jax.ShapeDtypeStruct((batch, B_KV_HEADS, seq // tk, VT_ROWS, tk), BF16)
    return pl.pallas_call(
        _kv_prep_kernel,
        grid=(batch, B_KV_HEADS),
        in_specs=[src(i) for i in range(6)] + [full(k_gain), full(cmp_pos), full(w1), full(b1), full(w2)],
        out_specs=[spec4(groups, HEAD_DIM), spec4(HEAD_DIM, groups), spec4(seq, KEY_COLS), vt_spec,
                   spec4(seq, KEY_COLS), vt_spec],
        out_shape=[shape4(groups, HEAD_DIM), shape4(HEAD_DIM, groups), shape4(seq, KEY_COLS), vt_shape,
                   shape4(seq, KEY_COLS), vt_shape],
        compiler_params=_params("parallel", "parallel"),
        name="kv_prep",
    )(*[a for a, _ in kv_srcs], k_gain, cmp_pos, w1, b1, w2)


def _nsa_kernel(slopes_ref, q_ref, bg_ref, gl_ref, kc_ref, vct_ref, ks_ref, vst_ref, kw_ref, vwt_ref,
                qgain_ref, ovt_ref, o_ref, qt_ref, m_ref, acc_ref, *, n_cmp):
    tq = q_ref.shape[0]
    n_grp = kc_ref.shape[0]
    n_slc = ovt_ref.shape[0]
    h = pl.program_id(1)
    i = pl.program_id(2)
    t0 = i * tq

    for g in range(B_GROUP):
        q = q_ref[:, g * HEAD_DIM:(g + 1) * HEAD_DIM]
        qt_ref[g, :HEAD_DIM, :] = (_rms(q, qgain_ref[...]) * (HEAD_DIM ** -0.5)).T.astype(BF16)

    n_idx = lax.broadcasted_iota(jnp.int32, (n_grp, tq), 0)
    t_idx = t0 + lax.broadcasted_iota(jnp.int32, (n_grp, tq), 1)
    dist_c = t_idx - (n_idx * CMP_STRIDE + (CMP_LEN - 1))
    valid_c = (n_idx < n_cmp) & (dist_c >= 0)
    dist_cf = dist_c.astype(F32)
    groups = range(B_GROUP)
    s_c = [_dot(kc_ref[...], qt_ref[g, :HEAD_DIM, :]) for g in groups]
    s_c = [jnp.where(valid_c, s - slopes_ref[h * B_GROUP + g] * dist_cf, NEG_INF) for g, s in zip(groups, s_c)]
    m_c = [jnp.max(s, axis=0, keepdims=True) for s in s_c]
    e_c = [jnp.where(valid_c, jnp.exp(s - m), 0.0) for s, m in zip(s_c, m_c)]
    p_c = [e / jnp.maximum(jnp.sum(e, axis=0, keepdims=True), 1e-30) for e in e_c]
    o_cmp = [_dot(vct_ref[...], p.astype(BF16)) for p in p_c]
    p_sum = sum(p_c[1:], p_c[0])

    imp = jnp.dot(ovt_ref[...], p_sum, precision=lax.Precision.HIGHEST, preferred_element_type=F32)
    j_idx = lax.broadcasted_iota(jnp.int32, (n_slc, tq), 0)
    j_f = j_idx.astype(F32)
    t_sel = t0 + lax.broadcasted_iota(jnp.int32, (n_slc, tq), 1)
    cur = t_sel // SLC_BLOCK
    forced = (j_idx == 0) | (j_idx == cur) | (j_idx == cur - 1)
    val = jnp.where(forced, FORCE_SCORE, jnp.where(j_idx * SLC_BLOCK <= t_sel, imp, -1.0))
    sel_t = jnp.zeros((n_slc, tq), F32)
    for _ in range(min(TOP_N, n_slc)):
        best = jnp.max(val, axis=0, keepdims=True)
        first = jnp.min(jnp.where(val == best, j_f, float(n_slc)), axis=0, keepdims=True)
        hit = j_f == first
        sel_t = jnp.where(hit, 1.0, sel_t)
        val = jnp.where(hit, -jnp.inf, val)

    unselected = (sel_t - 1.0) * (-NEG_INF)
    e_row = lax.broadcasted_iota(jnp.int32, (EXTRA_BLOCK0, tq), 0)
    pad = jnp.zeros((HEAD_DIM - EXTRA_BLOCK0 - n_slc, tq), F32)
    for g in range(B_GROUP):
        slope = slopes_ref[h * B_GROUP + g]
        alibi = jnp.where(e_row == 0, slope * ALIBI_SPLIT, jnp.where(e_row == 1, slope, 0.0))
        qt_ref[g, HEAD_DIM:, :] = jnp.concatenate([alibi, unselected, pad], axis=0).astype(BF16)

    key_r = lax.broadcasted_iota(jnp.int32, (tq, tq), 0)
    qry_c = lax.broadcasted_iota(jnp.int32, (tq, tq), 1)
    causal = key_r <= qry_c

    SLC, WIN = 0, 1
    branch_refs = {SLC: (ks_ref, vst_ref), WIN: (kw_ref, vwt_ref)}

    def tiles_update(tiles, first):
        heads = range(B_GROUP)
        s = []
        for br, kt, mask in tiles:
            k_blk = branch_refs[br][0][pl.ds(pl.multiple_of(kt * tq, tq), tq), :]
            s.append([_dot(k_blk, qt_ref[g]) for g in heads])
        s = [[sg if mask is None else jnp.where(mask, sg, NEG_INF) for sg in st]
             for st, (_, _, mask) in zip(s, tiles)]
        branches = sorted({br for br, _, _ in tiles})
        m_old, m_new = {}, {}
        for br in branches:
            for g in heads:
                m = None
                for st, (tb, _, _) in zip(s, tiles):
                    if tb == br:
                        tile_max = jnp.max(st[g], axis=0, keepdims=True)
                        m = tile_max if m is None else jnp.maximum(m, tile_max)
                if not first:
                    m_old[br, g] = m_ref[br * B_GROUP + g]
                    m = jnp.maximum(m_old[br, g], m)
                m_new[br, g] = m
        p = [[jnp.exp(st[g] - m_new[br, g]).astype(BF16) for g in heads] for st, (br, _, _) in zip(s, tiles)]
        pv = [[_dot(branch_refs[br][1][kt], pt[g]) for g in heads] for pt, (br, kt, _) in zip(p, tiles)]
        for br in branches:
            for g in heads:
                total = None
                for pvt, (tb, _, _) in zip(pv, tiles):
                    if tb == br:
                        total = pvt[g] if total is None else total + pvt[g]
                slot = br * B_GROUP + g
                if first:
                    acc_ref[slot] = total
                else:
                    acc_ref[slot] = jnp.exp(m_old[br, g] - m_new[br, g]) * acc_ref[slot] + total
                m_ref[slot] = m_new[br, g]

    def finish(br):
        return [acc_ref[br * B_GROUP + g, :HEAD_DIM, :] / acc_ref[br * B_GROUP + g, HEAD_DIM:HEAD_DIM + 1, :]
                for g in range(B_GROUP)]

    win_tiles = (WINDOW + tq - 1) // tq

    def win_tile(off):
        needs_mask = off * tq + tq - 1 >= WINDOW
        return (WIN, i - off, ((qry_c - key_r) + off * tq < WINDOW) if needs_mask else None)

    for count in range(win_tiles + 1):
        @pl.when(jnp.minimum(i, win_tiles) == count)
        def _(count=count):
            tiles_update([(SLC, i, causal), (WIN, i, causal)] + [win_tile(off) for off in range(1, count + 1)],
                         True)

    def slc_group(j, carry):
        tiles_update([(SLC, SLC_GROUP * j + u, None) for u in range(SLC_GROUP)], False)
        return carry

    lax.fori_loop(0, i // SLC_GROUP, slc_group, 0)
    for rest in range(1, SLC_GROUP):
        @pl.when(i % SLC_GROUP == rest)
        def _(rest=rest):
            first_tile = i // SLC_GROUP * SLC_GROUP
            tiles_update([(SLC, first_tile + u, None) for u in range(rest)], False)

    o_slc = finish(SLC)
    o_win = finish(WIN)

    gates_t = jax.nn.sigmoid(gl_ref[...].T)
    for g in range(B_GROUP):
        c = slice(g * HEAD_DIM, (g + 1) * HEAD_DIM)
        o_t = (gates_t[g:g + 1] * o_cmp[g]
               + gates_t[B_GROUP + g:B_GROUP + g + 1] * o_slc[g]
               + gates_t[2 * B_GROUP + g:2 * B_GROUP + g + 1] * o_win[g])
        o_ref[:, c] = (o_t.T * jax.nn.silu(bg_ref[:, c])).astype(BF16)


def _overlap_t(n_grp, n_cmp, n_slc):
    cs = np.arange(n_grp)[None, :] * CMP_STRIDE
    ss = np.arange(n_slc)[:, None] * SLC_BLOCK
    ov = np.clip(np.minimum(cs + CMP_LEN, ss + SLC_BLOCK) - np.maximum(cs, ss), 0, None) / CMP_LEN
    ov = ov * (np.arange(n_grp)[None, :] < n_cmp)
    return jnp.asarray(ov, dtype=F32)


def _nsa_attn(z, zg, kc, vct, ks, vst, kw, vwt, q_gain, batch, seq, q_col0, bg_col0, zg_col0):
    t = z.shape[0]
    tq = min(ATT_TQ, seq)
    assert seq % tq == 0 and tq % SLC_BLOCK == 0 and seq // SLC_BLOCK <= LANES
    nq = seq // tq
    n_grp = seq // CMP_STRIDE
    n_cmp = (seq - CMP_LEN) // CMP_STRIDE + 1
    n_slc = seq // SLC_BLOCK
    gw = B_GROUP * HEAD_DIM
    h_idx = jnp.arange(1, B_HEADS + 1, dtype=F32)
    slopes = jnp.exp2(-8.0 * h_idx / B_HEADS)
    ovt = _overlap_t(n_grp, n_cmp, n_slc)
    rows = lambda b, h, i: b * nq + i
    spec4 = lambda r, c: pl.BlockSpec((None, None, r, c), lambda b, h, i: (b, h, 0, 0))
    vt_spec = pl.BlockSpec((None, None, nq, VT_ROWS, tq), lambda b, h, i: (b, h, 0, 0, 0))
    full = lambda a: pl.BlockSpec(a.shape, lambda b, h, i: (0,) * a.ndim)
    qg = q_gain.reshape(1, HEAD_DIM)
    return pl.pallas_call(
        functools.partial(_nsa_kernel, n_cmp=n_cmp),
        grid=(batch, B_KV_HEADS, nq),
        in_specs=[
            pl.BlockSpec(memory_space=pltpu.SMEM),
            pl.BlockSpec((tq, gw), lambda b, h, i: (rows(b, h, i), q_col0 + h)),
            pl.BlockSpec((tq, gw), lambda b, h, i: (rows(b, h, i), bg_col0 + h)),
            pl.BlockSpec((tq, LANES), lambda b, h, i: (rows(b, h, i), zg_col0 + h)),
            spec4(n_grp, HEAD_DIM), spec4(HEAD_DIM, n_grp), spec4(seq, KEY_COLS), vt_spec,
            spec4(seq, KEY_COLS), vt_spec, full(qg), full(ovt),
        ],
        out_specs=pl.BlockSpec((tq, gw), lambda b, h, i: (rows(b, h, i), h)),
        out_shape=jax.ShapeDtypeStruct((t, B_WIDTH), BF16),
        scratch_shapes=[
            pltpu.VMEM((B_GROUP, KEY_COLS, tq), BF16),
            pltpu.VMEM((2 * B_GROUP, 1, tq), F32),
            pltpu.VMEM((2 * B_GROUP, VT_ROWS, tq), F32),
        ],
        compiler_params=_params("parallel", "parallel", "arbitrary"),
        name="nsa_attn",
    )(slopes, z, z, zg, kc, vct, ks, vst, kw, vwt, qg, ovt)


def _hgrn_kernel(q_ref, f_ref, i_ref, gate_ref, lb_ref, og_ref, o_ref, state_ref, *, layer):
    n_rows = q_ref.shape[0]
    n_heads = q_ref.shape[1] // C_DIM
    mid = C_CHUNK // 2 - 1

    logits = lb_ref[...]
    e = jnp.exp(logits - jnp.max(logits, axis=0, keepdims=True))
    p = e / jnp.sum(e, axis=0, keepdims=True)
    lb = jnp.sum(p[:layer + 1], axis=0, keepdims=True) - p[0:1]

    r_i = lax.broadcasted_iota(jnp.int32, (C_CHUNK, C_CHUNK), 0)
    c_i = lax.broadcasted_iota(jnp.int32, (C_CHUNK, C_CHUNK), 1)
    lower = r_i >= c_i
    tri = jnp.where(lower, 1.0, 0.0).astype(BF16)

    @pl.when(pl.program_id(2) == 0)
    def _():
        state_ref[...] = jnp.zeros_like(state_ref)

    def body(c, carry):
        r0 = pl.multiple_of(c * C_CHUNK, C_CHUNK)
        rows = pl.ds(r0, C_CHUNK)
        f = lb + (1.0 - lb) * jax.nn.sigmoid(f_ref[rows, :])
        k = 1.0 - f
        g = jnp.log(f)
        g_hi = g.astype(BF16)
        g_rest = g - g_hi.astype(F32)
        g_mid = g_rest.astype(BF16)
        g_lo = (g_rest - g_mid.astype(F32)).astype(BF16)
        gcum = _dot(tri, g_hi) + _dot(tri, g_mid) + _dot(tri, g_lo)
        g_mid = gcum[mid:mid + 1, :]
        g_end = gcum[C_CHUNK - 1:C_CHUNK, :]
        qa = q_ref[rows, :] * jnp.exp(gcum - g_mid)
        kb = k * jnp.exp(g_mid - gcum)
        q_in = (qa * jnp.exp(g_mid)).astype(BF16)
        k_st = (kb * jnp.exp(g_end - g_mid)).astype(BF16)
        qa = qa.astype(BF16)
        kb = kb.astype(BF16)
        v = i_ref[rows, :].astype(BF16)
        decay = jnp.exp(g_end)
        heads = range(n_heads)
        cs = [slice(hh * C_DIM, (hh + 1) * C_DIM) for hh in heads]
        attn = [_dot_nt(qa[:, c], kb[:, c]) for c in cs]
        state = [state_ref[hh] for hh in heads]
        inter = [_dot(q_in[:, c], s.astype(BF16)) for c, s in zip(cs, state)]
        update = [_dot_tn(k_st[:, c], v[:, c]) for c in cs]
        attn = [jnp.where(lower, a, 0.0).astype(BF16) for a in attn]
        o = [_dot(a, v[:, c]) + x for a, c, x in zip(attn, cs, inter)]
        for hh in heads:
            decay_rows = jnp.broadcast_to(decay[:, cs[hh]], (C_DIM, C_DIM)).T
            state_ref[hh] = decay_rows * state[hh] + update[hh]
        for hh in heads:
            out = _rms(o[hh], og_ref[...]) * jax.nn.silu(gate_ref[rows, cs[hh]])
            o_ref[rows, cs[hh]] = out.astype(BF16)
        return carry

    lax.fori_loop(0, n_rows // C_CHUNK, body, 0, unroll=16)


def _hgrn(z, lb_logits, out_gain, batch, seq, layer):
    t = z.shape[0]
    width = C_HEADS * C_DIM
    bw = HGRN_HEADS * C_DIM
    nb = width // bw
    rows = min(HGRN_ROWS, seq)
    ns = seq // rows
    assert seq % rows == 0 and rows % C_CHUNK == 0
    depth = lb_logits.shape[0]
    col = lambda part: pl.BlockSpec((rows, bw), lambda b, j, s, part=part: (b * ns + s, part * nb + j))
    og = out_gain.reshape(1, C_DIM)
    return pl.pallas_call(
        functools.partial(_hgrn_kernel, layer=layer),
        grid=(batch, nb, ns),
        in_specs=[col(0), col(1), col(2), col(3),
                  pl.BlockSpec((depth, bw), lambda b, j, s: (0, j)),
                  pl.BlockSpec((1, C_DIM), lambda b, j, s: (0, 0))],
        out_specs=pl.BlockSpec((rows, bw), lambda b, j, s: (b * ns + s, j)),
        out_shape=jax.ShapeDtypeStruct((t, width), BF16),
        scratch_shapes=[pltpu.VMEM((HGRN_HEADS, C_DIM, C_DIM), F32)],
        compiler_params=_params("parallel", "parallel", "arbitrary"),
        name="hgrn",
    )(z, z, z, z, lb_logits, og)


def _even_layer(h, batch, seq, norm_g, w_in, ln_g, ln_b, sgu_w, sgu_b, q_gain, k_gain,
                cmp_pos, cmp_w1, cmp_b1, cmp_w2, w_out, next_gain):
    d = h.shape[1]
    n_main = 3 * A_WIDTH + 2 * B_WIDTH + 6 * B_KV_HEADS * HEAD_DIM
    wg = w_in[:, n_main:].reshape(d, 3, B_KV_HEADS, B_GROUP).transpose(0, 2, 1, 3)
    wg = wg.reshape(d, B_KV_HEADS, 3 * B_GROUP)
    wg = jnp.pad(wg, ((0, 0), (0, 0), (0, LANES - 3 * B_GROUP))).reshape(d, B_KV_HEADS * LANES)
    w_all = jnp.concatenate([w_in[:, :n_main].astype(BF16), wg.astype(BF16)], axis=1)
    z = _matmul(_rms_norm(h, norm_g), w_all)

    sgu_b_full = jnp.broadcast_to(sgu_b[:, :, None], (A_GROUPS, A_CHUNK, HEAD_DIM))
    a_out = _mixer_a(z, ln_g, ln_b, sgu_w, sgu_b_full)

    gw = B_GROUP * HEAD_DIM
    q_col0 = 3 * A_WIDTH // gw
    bg_col0 = (3 * A_WIDTH + B_WIDTH) // gw
    kv_col0 = (3 * A_WIDTH + 2 * B_WIDTH) // HEAD_DIM
    kv_srcs = [(z, kv_col0 + i * B_KV_HEADS) for i in range(6)]
    zg_col0 = n_main // LANES
    w1 = cmp_w1.reshape(2, CMP_LEN, HEAD_DIM, CMP_HIDDEN).astype(BF16)
    kc, vct, ks, vst, kw, vwt = _kv_prep(kv_srcs, batch, seq, k_gain, cmp_pos, w1,
                                         cmp_b1.reshape(2, 1, CMP_HIDDEN), cmp_w2.astype(BF16))
    b_out = _nsa_attn(z, z, kc, vct, ks, vst, kw, vwt, q_gain, batch, seq, q_col0, bg_col0, zg_col0)

    w_out = w_out.astype(BF16)
    return _matmul_residual(h, [a_out, b_out], [w_out[:A_WIDTH], w_out[A_WIDTH:]], next_gain)


def _odd_layer(h, hn, batch, seq, layer, norm_g, w_in, lb_logits, out_gain, w_out, next_gain):
    w_in = w_in.astype(BF16)
    z = _matmul(hn if hn is not None else _rms_norm(h, norm_g), w_in)
    o = _hgrn(z, lb_logits, out_gain, batch, seq, layer)
    return _matmul_residual(h, [o], [w_out.astype(BF16)], next_gain)


def kernel(x, even_norm, even_w_in, sgu_ln_g, sgu_ln_b, sgu_w, sgu_b, nsa_q_gain, nsa_k_gain, cmp_pos, cmp_w1, cmp_b1, cmp_w2, even_w_out, odd_norm, odd_w_in, hgrn_lb, hgrn_out_gain, odd_w_out):
    batch, seq, d = x.shape
    depth = hgrn_lb.shape[0]
    h = x.reshape(batch * seq, d)
    hn = None
    for layer in range(depth):
        next_gain = odd_norm[(layer + 1) // 2] if (layer + 1 < depth and layer % 2 == 0) else None
        if layer % 2 == 0:
            e = layer // 2
            h, hn = _even_layer(h, batch, seq, even_norm[e], even_w_in[e], sgu_ln_g[e], sgu_ln_b[e],
                                sgu_w[e], sgu_b[e], nsa_q_gain[e], nsa_k_gain[e], cmp_pos[e], cmp_w1[e],
                                cmp_b1[e], cmp_w2[e], even_w_out[e], next_gain)
        else:
            o = layer // 2
            h, hn = _odd_layer(h, hn, batch, seq, layer, odd_norm[o], odd_w_in[o], hgrn_lb,
                               hgrn_out_gain[o], odd_w_out[o], next_gain)
    return h.reshape(batch, seq, d)
```
